```python
import jax, jax.numpy as jnp
from jax import lax
import numpy as np

D_MODEL = 1024
BATCH = 16
SEQ = 2048
DEPTH = 1
DEC_BATCH = 32
DEC_SEQ = 1
PAST_LEN = 16384
PAGE_SIZE = 128

FOX_HEADS = 8
FOX_HD = 64
FOX_W = FOX_HEADS * FOX_HD
GLA_HEADS = 4
GLA_DK = 64
GLA_DV = 128
GLA_K_W = GLA_HEADS * GLA_DK
GLA_V_W = GLA_HEADS * GLA_DV
GLA_RANK = 16
GLA_GATE_NORM = 16.0
GLA_CHUNK = 64
Q_BLOCK = 128
D_FF = 4 * D_MODEL
EPS = 1e-6
D_IN = 3 * FOX_W + FOX_HEADS + 2 * GLA_K_W + 2 * GLA_V_W + GLA_RANK + 2 * D_MODEL

kernel_name = 'fox_gla_gated_hybrid_step'

F32 = jnp.float32


def _rmsnorm(x, g):
    xf = x.astype(F32)
    y = xf * lax.rsqrt(jnp.mean(xf * xf, axis=-1, keepdims=True) + EPS)
    return (y * g.astype(F32)).astype(x.dtype)


def _mixer_inputs(h, w_in, b_f, w_alpha_up, b_alpha):
    B, L, _ = h.shape
    sizes = (FOX_W, FOX_W, FOX_W, FOX_HEADS, GLA_K_W, GLA_K_W, GLA_V_W, GLA_V_W, GLA_RANK, D_MODEL, D_MODEL)
    idx = [int(i) for i in np.cumsum(sizes)[:-1]]
    qf, kf, vf, fl, qg, kg, vg, og, lr, gfox, ggla = jnp.split(h @ w_in, idx, axis=-1)
    q_fox = qf.reshape(B, L, FOX_HEADS, FOX_HD)
    k_fox = kf.reshape(B, L, FOX_HEADS, FOX_HD)
    v_fox = vf.reshape(B, L, FOX_HEADS, FOX_HD)
    logf = jax.nn.log_sigmoid(fl.astype(F32) + b_f.astype(F32))
    q_gla = qg.reshape(B, L, GLA_HEADS, GLA_DK) * (GLA_DK ** -0.5)
    k_gla = kg.reshape(B, L, GLA_HEADS, GLA_DK)
    v_gla = vg.reshape(B, L, GLA_HEADS, GLA_DV)
    log_alpha = (jax.nn.log_sigmoid((lr @ w_alpha_up + b_alpha).astype(F32)) / GLA_GATE_NORM
                 ).reshape(B, L, GLA_HEADS, GLA_DK)
    return q_fox, k_fox, v_fox, logf, q_gla, k_gla, v_gla, log_alpha, og, gfox, ggla


def _fox_prompt(q, k, v, logf):
    B, S, H, Dh = q.shape
    nb = S // Q_BLOCK
    scale = Dh ** -0.5
    F = jnp.cumsum(logf, axis=1)
    Fk = jnp.swapaxes(F, 1, 2)
    qb = jnp.moveaxis(q.reshape(B, nb, Q_BLOCK, H, Dh), 1, 0)
    Fq = jnp.moveaxis(jnp.swapaxes(F, 1, 2).reshape(B, H, nb, Q_BLOCK), 2, 0)
    starts = jnp.arange(nb) * Q_BLOCK
    key_pos = jnp.arange(S)

    def block(args):
        qi, Fi, s0 = args
        logits = jnp.einsum('bqhd,bkhd->bhqk', qi, k).astype(F32) * scale
        logits = logits + Fi[..., None] - Fk[:, :, None, :]
        qpos = s0 + jnp.arange(Q_BLOCK)
        logits = jnp.where(key_pos[None, :] <= qpos[:, None], logits, -jnp.inf)
        p = jax.nn.softmax(logits, axis=-1)
        return jnp.einsum('bhqk,bkhd->bqhd', p, v.astype(F32))

    o = lax.map(block, (qb, Fq, starts))
    return jnp.moveaxis(o, 0, 1).reshape(B, S, H * Dh)


def _fox_sample(q, k_new, v_new, logf_new, k_past, v_past, logf_past):
    B, L, H, Dh = q.shape
    scale = Dh ** -0.5
    lfp = logf_past.astype(F32)
    suffix = lax.cumsum(lfp, axis=1, reverse=True) - lfp
    cn = jnp.swapaxes(jnp.cumsum(logf_new, axis=1), 1, 2)
    lp = (jnp.einsum('bqhd,bkhd->bhqk', q, k_past).astype(F32) * scale
          + jnp.swapaxes(suffix, 1, 2)[:, :, None, :] + cn[..., None])
    ln = (jnp.einsum('bqhd,bkhd->bhqk', q, k_new).astype(F32) * scale
          + cn[..., :, None] - cn[..., None, :])
    causal = jnp.tril(jnp.ones((L, L), dtype=bool))
    ln = jnp.where(causal, ln, -jnp.inf)
    m = jnp.maximum(lp.max(-1, keepdims=True), ln.max(-1, keepdims=True))
    pp = jnp.exp(lp - m)
    pn = jnp.exp(ln - m)
    den = pp.sum(-1) + pn.sum(-1)
    num = (jnp.einsum('bhqk,bkhd->bqhd', pp, v_past.astype(F32))
           + jnp.einsum('bhqk,bkhd->bqhd', pn, v_new.astype(F32)))
    o = num / jnp.swapaxes(den, 1, 2)[..., None]
    return o.reshape(B, L, H * Dh)


def _gla_chunk(S0, q, k, v, la):
    S0 = S0.astype(F32)
    qf, kf, vf = q.astype(F32), k.astype(F32), v.astype(F32)
    L = q.shape[1]
    b = jnp.cumsum(la, axis=1)
    inter = jnp.einsum('blhd,bhde->blhe', qf * jnp.exp(b), S0)
    causal = jnp.tril(jnp.ones((L, L), dtype=bool))[None, :, :, None, None]
    diff = b[:, :, None] - b[:, None, :]
    decay = jnp.where(causal, jnp.exp(jnp.where(causal, diff, 0.0)), 0.0)
    A = jnp.einsum('bthd,bshd,btshd->bhts', qf, kf, decay)
    intra = jnp.einsum('bhts,bshe->bthe', A, vf)
    b_last = b[:, -1]
    S_new = (jnp.exp(b_last)[..., None] * S0
             + jnp.einsum('bshd,bshe->bhde', kf * jnp.exp(b_last[:, None] - b), vf))
    return S_new, inter + intra


def _gla_prompt(q, k, v, la):
    B, S, H, DK = q.shape
    nc = S // GLA_CHUNK

    def to_chunks(a):
        return jnp.moveaxis(a.reshape(B, nc, GLA_CHUNK, *a.shape[2:]), 1, 0)

    def step(state, inp):
        qc, kc, vc, lc = inp
        return _gla_chunk(state, qc, kc, vc, lc)

    S0 = jnp.zeros((B, H, DK, GLA_DV), F32)
    S_fin, o = lax.scan(step, S0, (to_chunks(q), to_chunks(k), to_chunks(v), to_chunks(la)))
    return S_fin, jnp.moveaxis(o, 0, 1).reshape(B, S, H, GLA_DV)


def _gla_out(o, og, gain):
    B, L = o.shape[:2]
    y = _rmsnorm(o, gain) * jax.nn.silu(og.astype(F32).reshape(B, L, GLA_HEADS, GLA_DV))
    return y.reshape(B, L, GLA_V_W)


def _finish(x, o_fox, o_gla, gfox, ggla, w_fox_out, w_gla_out, w_o, g_post_mix,
            g_pre_mlp, w_up, w_down, g_post_mlp):
    dt = x.dtype
    u = (jax.nn.sigmoid(gfox) * (o_fox.astype(dt) @ w_fox_out)
         + jax.nn.sigmoid(ggla) * (o_gla.astype(dt) @ w_gla_out))
    x = x + _rmsnorm(u @ w_o, g_post_mix)
    h = _rmsnorm(x, g_pre_mlp)
    m = jnp.square(jax.nn.relu(h @ w_up)) @ w_down
    return x + _rmsnorm(m, g_post_mlp)


def setup_inputs(seed: int = 0) -> dict:
    key = jax.random.key(seed)
    ks = jax.random.split(key, 24)
    n_pages = PAST_LEN // PAGE_SIZE
    n_used = DEC_BATCH * n_pages
    n_pool = n_used + max(1, n_used // 4)

    def nrm(k, shape, s):
        return jax.random.normal(k, shape, F32) * s

    def gain(k, n):
        return 1.0 + nrm(k, (DEPTH, n), 0.05)

    page_table = jax.random.permutation(ks[0], n_pool)[:n_used].reshape(DEC_BATCH, n_pages).astype(jnp.int32)
    return {
        'x_prompt': nrm(ks[1], (BATCH, SEQ, D_MODEL), 1.0),
        'x_sample': nrm(ks[2], (DEC_BATCH, DEC_SEQ, D_MODEL), 1.0),
        'cache_k': nrm(ks[3], (DEPTH, n_pool, PAGE_SIZE, FOX_HEADS, FOX_HD), 1.0),
        'cache_v': nrm(ks[4], (DEPTH, n_pool, PAGE_SIZE, FOX_HEADS, FOX_HD), 1.0),
        'cache_logf': jax.nn.log_sigmoid(nrm(ks[5], (DEPTH, n_pool, PAGE_SIZE, FOX_HEADS), 1.0) + 4.0),
        'state_gla': nrm(ks[6], (DEPTH, DEC_BATCH, GLA_HEADS, GLA_DK, GLA_DV), 0.3),
        'page_table': page_table,
        'g_pre_mix': gain(ks[7], D_MODEL),
        'w_in': nrm(ks[8], (DEPTH, D_MODEL, D_IN), D_MODEL ** -0.5),
        'b_f': jax.random.uniform(ks[9], (DEPTH, FOX_HEADS), F32, 2.0, 6.0),
        'w_alpha_up': nrm(ks[10], (DEPTH, GLA_RANK, GLA_K_W), GLA_RANK ** -0.5),
        'b_alpha': nrm(ks[11], (DEPTH, GLA_K_W), 0.1),
        'g_gla_norm': gain(ks[12], GLA_DV),
        'w_fox_out': nrm(ks[13], (DEPTH, FOX_W, D_MODEL), FOX_W ** -0.5),
        'w_gla_out': nrm(ks[14], (DEPTH, GLA_V_W, D_MODEL), GLA_V_W ** -0.5),
        'w_o': nrm(ks[15], (DEPTH, D_MODEL, D_MODEL), D_MODEL ** -0.5),
        'g_post_mix': gain(ks[16], D_MODEL),
        'g_pre_mlp': gain(ks[17], D_MODEL),
        'w_up': nrm(ks[18], (DEPTH, D_MODEL, D_FF), D_MODEL ** -0.5),
        'w_down': nrm(ks[19], (DEPTH, D_FF, D_MODEL), D_FF ** -0.5),
        'g_post_mlp': gain(ks[20], D_MODEL),
    }


def reference(x_prompt, x_sample, cache_k, cache_v, cache_logf, state_gla, page_table,
              g_pre_mix, w_in, b_f, w_alpha_up, b_alpha, g_gla_norm, w_fox_out, w_gla_out,
              w_o, g_post_mix, g_pre_mlp, w_up, w_down, g_post_mlp):
    xp, xs = x_prompt, x_sample
    DB, NP = page_table.shape
    kp_l, vp_l, fp_l, sp_l, ks_l, vs_l, fs_l, ss_l = [], [], [], [], [], [], [], []
    for l in range(DEPTH):
        hp = _rmsnorm(xp, g_pre_mix[l])
        qf, kf, vf, lf, qg, kg, vg, la, og, gfx, ggl = _mixer_inputs(hp, w_in[l], b_f[l], w_alpha_up[l], b_alpha[l])
        o_fox = _fox_prompt(qf, kf, vf, lf)
        S_p, o_g = _gla_prompt(qg, kg, vg, la)
        o_gla = _gla_out(o_g, og, g_gla_norm[l])
        xp = _finish(xp, o_fox, o_gla, gfx, ggl, w_fox_out[l], w_gla_out[l], w_o[l], g_post_mix[l],
                     g_pre_mlp[l], w_up[l], w_down[l], g_post_mlp[l])
        kp_l.append(kf); vp_l.append(vf); fp_l.append(lf); sp_l.append(S_p)
        hs = _rmsnorm(xs, g_pre_mix[l])
        qf, kf, vf, lf, qg, kg, vg, la, og, gfx, ggl = _mixer_inputs(hs, w_in[l], b_f[l], w_alpha_up[l], b_alpha[l])
        k_past = cache_k[l][page_table].reshape(DB, NP * PAGE_SIZE, FOX_HEADS, FOX_HD)
        v_past = cache_v[l][page_table].reshape(DB, NP * PAGE_SIZE, FOX_HEADS, FOX_HD)
        f_past = cache_logf[l][page_table].reshape(DB, NP * PAGE_SIZE, FOX_HEADS)
        o_fox = _fox_sample(qf, kf, vf, lf, k_past, v_past, f_past)
        S_s, o_g = _gla_chunk(state_gla[l], qg, kg, vg, la)
        o_gla = _gla_out(o_g, og, g_gla_norm[l])
        xs = _finish(xs, o_fox, o_gla, gfx, ggl, w_fox_out[l], w_gla_out[l], w_o[l], g_post_mix[l],
                     g_pre_mlp[l], w_up[l], w_down[l], g_post_mlp[l])
        ks_l.append(kf); vs_l.append(vf); fs_l.append(lf); ss_l.append(S_s)
    k_prompt, v_prompt = jnp.stack(kp_l), jnp.stack(vp_l)
    logf_prompt, gla_state_prompt = jnp.stack(fp_l), jnp.stack(sp_l)
    k_sample, v_sample = jnp.stack(ks_l), jnp.stack(vs_l)
    logf_sample, gla_state_sample = jnp.stack(fs_l), jnp.stack(ss_l)
    return (xp, xs, k_prompt, v_prompt, logf_prompt, gla_state_prompt,
            k_sample, v_sample, logf_sample, gla_state_sample)
```

```python
import functools

import jax
import jax.numpy as jnp
from jax import lax
from jax.experimental import pallas as pl
from jax.experimental.pallas import tpu as pltpu

F32 = jnp.float32
BF16 = jnp.bfloat16

FOX_HEADS = 8
FOX_HD = 64
FOX_W = FOX_HEADS * FOX_HD
GLA_HEADS = 4
GLA_DK = 64
GLA_DV = 128
GLA_K_W = GLA_HEADS * GLA_DK
GLA_V_W = GLA_HEADS * GLA_DV
GLA_RANK = 16
GLA_GATE_NORM = 16.0
GLA_CHUNK = 64
EPS = 1e-6
LANES = 128
VMEM_LIMIT = 56 * 1024 * 1024

ROW_TILE = 512
ATT_TILE = 256
PAGES_PER_STEP = 8


def _params(n_axes):
    return pltpu.CompilerParams(dimension_semantics=("arbitrary",) * n_axes,
                                vmem_limit_bytes=VMEM_LIMIT)


def _log_sigmoid(z):
    return jnp.minimum(z, 0.0) - jnp.log1p(jnp.exp(-jnp.abs(z)))


def _sigmoid(z):
    return 1.0 / (1.0 + jnp.exp(-z))


def _rms(x, g):
    return x * lax.rsqrt(jnp.mean(x * x, axis=-1, keepdims=True) + EPS) * g


def _dot(a, b):
    return jnp.dot(a, b, preferred_element_type=F32)


def _dot_nt(a, b):
    return lax.dot_general(a, b, (((1,), (1,)), ((), ())), preferred_element_type=F32)


def _in_proj_body(x_ref, g_ref, wqkv_ref, wgla_ref, wsm_ref, wau_ref, bal_ref, bf_ref,
                  kT_ref, vT_ref, kTb_ref, qb_ref, vb_ref, lfT_ref, qg_ref, kg_ref, vg_ref, la_ref):
    h = _rms(x_ref[...], g_ref[...]).astype(BF16)
    qkv = _dot(h, wqkv_ref[...])
    q = qkv[:, :FOX_W] * (FOX_HD ** -0.5)
    k = qkv[:, FOX_W:2 * FOX_W]
    v = qkv[:, 2 * FOX_W:]
    qb_ref[...] = q.astype(BF16)
    vb_ref[...] = v.astype(BF16)
    kT = k.T
    kT_ref[...] = kT
    kTb_ref[...] = kT.astype(BF16)
    vT_ref[...] = v.T
    gl = _dot(h, wgla_ref[...])
    qg_ref[...] = gl[:, :GLA_K_W] * (GLA_DK ** -0.5)
    kg_ref[...] = gl[:, GLA_K_W:2 * GLA_K_W]
    vg_ref[...] = gl[:, 2 * GLA_K_W:].astype(BF16)
    sm = _dot(h, wsm_ref[...])
    lf = _log_sigmoid(sm + bf_ref[...])
    lfT_ref[...] = lf.T[:FOX_HEADS, :]
    z = _dot(sm.astype(BF16), wau_ref[...]) + bal_ref[...]
    la_ref[...] = _log_sigmoid(z) * (1.0 / GLA_GATE_NORM)


def _in_proj(x, g, wqkv, wgla, wsm, wau, bal, bfr, tm):
    B, S, D = x.shape
    nt = S // tm
    row = lambda w: pl.BlockSpec((None, tm, w), lambda b, i: (b, i, 0))
    col = lambda w: pl.BlockSpec((None, w, tm), lambda b, i: (b, 0, i))
    full = lambda a: pl.BlockSpec(a.shape, lambda b, i: (0,) * a.ndim)
    sds = jax.ShapeDtypeStruct
    return pl.pallas_call(
        _in_proj_body,
        grid=(B, nt),
        in_specs=[row(D), full(g), full(wqkv), full(wgla), full(wsm), full(wau), full(bal), full(bfr)],
        out_specs=[col(FOX_W), col(FOX_W), col(FOX_W), row(FOX_W), row(FOX_W), col(FOX_HEADS),
                   row(GLA_K_W), row(GLA_K_W), row(GLA_V_W), row(GLA_K_W)],
        out_shape=[sds((B, FOX_W, S), F32), sds((B, FOX_W, S), F32), sds((B, FOX_W, S), BF16),
                   sds((B, S, FOX_W), BF16), sds((B, S, FOX_W), BF16), sds((B, FOX_HEADS, S), F32),
                   sds((B, S, GLA_K_W), F32), sds((B, S, GLA_K_W), F32), sds((B, S, GLA_V_W), BF16),
                   sds((B, S, GLA_K_W), F32)],
        compiler_params=_params(2),
        name="in_proj",
    )(x, g, wqkv, wgla, wsm, wau, bal, bfr)


def _cumsum_lanes(x):
    n = x.shape[-1]
    lane = lax.broadcasted_iota(jnp.int32, x.shape, x.ndim - 1)
    sh = 1
    while sh < n:
        x = x + jnp.where(lane >= sh, pltpu.roll(x, sh, x.ndim - 1), 0.0)
        sh *= 2
    return x


def _fox_body(q_ref, kT_ref, v_ref, lf_ref, o_ref, F_ref, Fe_ref, Fo_ref, *, t):
    p = pl.program_id(1)
    i = pl.program_id(2)

    @pl.when((p == 0) & (i == 0))
    def _():
        F_ref[...] = _cumsum_lanes(lf_ref[...])

    @pl.when(i == 0)
    def _():
        F = F_ref[...]
        row = lax.broadcasted_iota(jnp.int32, F.shape, 0)
        Fe_ref[...] = jnp.sum(jnp.where(row == 2 * p, F, 0.0), axis=0, keepdims=True)
        Fo_ref[...] = jnp.sum(jnp.where(row == 2 * p + 1, F, 0.0), axis=0, keepdims=True)

    q = q_ref[...]
    lane = lax.broadcasted_iota(jnp.int32, q.shape, 1)
    zero = jnp.zeros_like(q)
    q2 = jnp.concatenate([jnp.where(lane < FOX_HD, q, zero), jnp.where(lane >= FOX_HD, q, zero)], axis=0)

    def block(j, carry, masked):
        m, l, acc = carry
        c0 = pl.multiple_of(j * t, t)
        s = _dot(q2, kT_ref[:, pl.ds(c0, t)])
        s = jnp.concatenate([s[:t] - Fe_ref[:, pl.ds(c0, t)], s[t:] - Fo_ref[:, pl.ds(c0, t)]], axis=0)
        if masked:
            r = lax.broadcasted_iota(jnp.int32, (2 * t, t), 0)
            r = jnp.where(r >= t, r - t, r)
            c = lax.broadcasted_iota(jnp.int32, (2 * t, t), 1)
            s = jnp.where(c <= r, s, -jnp.inf)
        m_new = jnp.maximum(m, jnp.max(s, axis=1, keepdims=True))
        alpha = jnp.exp(m - m_new)
        pr = jnp.exp(s - m_new)
        l = alpha * l + jnp.sum(pr, axis=1, keepdims=True)
        acc = alpha * acc + _dot(pr.astype(BF16), v_ref[pl.ds(c0, t), :])
        return m_new, l, acc

    init = (jnp.full((2 * t, 1), -jnp.inf, F32), jnp.zeros((2 * t, 1), F32), jnp.zeros((2 * t, LANES), F32))
    carry = lax.fori_loop(0, i, lambda j, c: block(j, c, False), init)
    m, l, acc = block(i, carry, True)
    out = acc / l
    lane_o = lax.broadcasted_iota(jnp.int32, (t, LANES), 1)
    o_ref[...] = jnp.where(lane_o < FOX_HD, out[:t], out[t:]).astype(o_ref.dtype)


def _fox_prompt(qb, kTb, vb, lfT, t):
    B, S, _ = qb.shape
    npair = FOX_HEADS // 2
    return pl.pallas_call(
        functools.partial(_fox_body, t=t),
        grid=(B, npair, S // t),
        in_specs=[pl.BlockSpec((None, t, LANES), lambda b, p, i: (b, i, p)),
                  pl.BlockSpec((None, LANES, S), lambda b, p, i: (b, p, 0)),
                  pl.BlockSpec((None, S, LANES), lambda b, p, i: (b, 0, p)),
                  pl.BlockSpec((None, FOX_HEADS, S), lambda b, p, i: (b, 0, 0))],
        out_specs=pl.BlockSpec((None, t, LANES), lambda b, p, i: (b, i, p)),
        out_shape=jax.ShapeDtypeStruct((B, S, FOX_W), BF16),
        scratch_shapes=[pltpu.VMEM((FOX_HEADS, S), F32), pltpu.VMEM((1, S), F32), pltpu.VMEM((1, S), F32)],
        compiler_params=_params(3),
        name="fox_prompt",
    )(qb, kTb, vb, lfT)


def _gla_body(qg_ref, kg_ref, vg_ref, la_ref, s0_ref, o_ref, sN_ref, st_ref, *, bt, C):
    c = pl.program_id(1)
    KW, VW = GLA_K_W, GLA_V_W
    blk = (lax.broadcasted_iota(jnp.int32, (VW, KW), 0) // GLA_DV
           == lax.broadcasted_iota(jnp.int32, (VW, KW), 1) // GLA_DK)

    @pl.when(c == 0)
    def _():
        for b in range(bt):
            s0 = s0_ref[b]
            rows = [jnp.concatenate([s0[h] if hh == h else jnp.zeros((GLA_DK, GLA_DV), F32)
                                     for hh in range(GLA_HEADS)], axis=1) for h in range(GLA_HEADS)]
            st_ref[b] = jnp.concatenate(rows, axis=0).T

    tri = (lax.broadcasted_iota(jnp.int32, (C, C), 1) <= lax.broadcasted_iota(jnp.int32, (C, C), 0))
    tri_b = tri.astype(BF16)
    tri4 = jnp.concatenate([tri] * GLA_HEADS, axis=0)
    head_of_lane = lax.broadcasted_iota(jnp.int32, (C, KW), 1) // GLA_DK

    def one(b, _):
        la = la_ref[b]
        hi = la.astype(BF16)
        r1 = la - hi.astype(F32)
        mid = r1.astype(BF16)
        lo = (r1 - mid.astype(F32)).astype(BF16)
        bc = _dot(tri_b, hi) + _dot(tri_b, mid) + _dot(tri_b, lo)
        ref = bc[C // 2 - 1:C // 2, :]
        last = bc[C - 1:C, :]
        q = qg_ref[b]
        k = kg_ref[b]
        v = vg_ref[b]
        st = st_ref[b]
        qi = (q * jnp.exp(bc)).astype(BF16)
        qt = q * jnp.exp(bc - ref)
        kt = (k * jnp.exp(ref - bc)).astype(BF16)
        kd = (k * jnp.exp(last - bc)).astype(BF16)
        qm = jnp.concatenate([jnp.where(head_of_lane == h, qt, 0.0) for h in range(GLA_HEADS)],
                             axis=0).astype(BF16)
        A = jnp.where(tri4, _dot_nt(qm, kt), 0.0).astype(BF16)
        intra = jnp.concatenate([_dot(A[h * C:(h + 1) * C], v[:, h * GLA_DV:(h + 1) * GLA_DV])
                                 for h in range(GLA_HEADS)], axis=1)
        inter = _dot_nt(qi, st.astype(BF16))
        o_ref[b] = inter + intra
        upd = _dot(v.astype(F32).T.astype(BF16), kd)
        st_ref[b] = jnp.where(blk, st * jnp.exp(last) + upd, 0.0)
        return 0

    lax.fori_loop(0, bt, one, 0)

    @pl.when(c == pl.num_programs(1) - 1)
    def _():
        for b in range(bt):
            s = st_ref[b].T
            for h in range(GLA_HEADS):
                sN_ref[b, h] = s[h * GLA_DK:(h + 1) * GLA_DK, h * GLA_DV:(h + 1) * GLA_DV]


def _gla(qg, kg, vg, la, s0, bt):
    B, S, _ = qg.shape
    C = GLA_CHUNK
    spec = lambda w: pl.BlockSpec((bt, C, w), lambda g, c: (g, c, 0))
    sspec = pl.BlockSpec((bt, GLA_HEADS, GLA_DK, GLA_DV), lambda g, c: (g, 0, 0, 0))
    return pl.pallas_call(
        functools.partial(_gla_body, bt=bt, C=C),
        grid=(B // bt, S // C),
        in_specs=[spec(GLA_K_W), spec(GLA_K_W), spec(GLA_V_W), spec(GLA_K_W), sspec],
        out_specs=[spec(GLA_V_W), sspec],
        out_shape=[jax.ShapeDtypeStruct((B, S, GLA_V_W), F32),
                   jax.ShapeDtypeStruct((B, GLA_HEADS, GLA_DK, GLA_DV), F32)],
        scratch_shapes=[pltpu.VMEM((bt, GLA_V_W, GLA_K_W), F32)],
        compiler_params=_params(2),
        name="gla_scan",
    )(qg, kg, vg, la, s0)


def _mix_body(x_ref, of_ref, og_ref, g1_ref, gn_ref, g2_ref, wog_ref, wgf_ref, wgg_ref, wfo_ref, wgo_ref, wo_ref,
              y_ref):
    x = x_ref[...]
    h = _rms(x, g1_ref[...]).astype(BF16)
    gate = _dot(h, wog_ref[...])
    o = og_ref[...]
    ys = []
    for hh in range(GLA_HEADS):
        sl = slice(hh * GLA_DV, (hh + 1) * GLA_DV)
        gt = gate[:, sl]
        ys.append(_rms(o[:, sl], gn_ref[...]) * (gt * _sigmoid(gt)))
    y_gla = jnp.concatenate(ys, axis=1).astype(BF16)
    u = (_sigmoid(_dot(h, wgf_ref[...])) * _dot(of_ref[...], wfo_ref[...])
         + _sigmoid(_dot(h, wgg_ref[...])) * _dot(y_gla, wgo_ref[...]))
    y_ref[...] = x + _rms(_dot(u.astype(BF16), wo_ref[...]), g2_ref[...])


def _mix(x, o_fox, o_gla, g1, gn, g2, wog, wgf, wgg, wfo, wgo, wo, tm):
    N, D = x.shape
    row = lambda w: pl.BlockSpec((tm, w), lambda i: (i, 0))
    full = lambda a: pl.BlockSpec(a.shape, lambda i: (0,) * a.ndim)
    return pl.pallas_call(
        _mix_body,
        grid=(N // tm,),
        in_specs=[row(D), row(FOX_W), row(GLA_V_W)] + [full(a) for a in (g1, gn, g2, wog, wgf, wgg, wfo, wgo, wo)],
        out_specs=row(D),
        out_shape=jax.ShapeDtypeStruct((N, D), F32),
        compiler_params=_params(1),
        name="mix",
    )(x, o_fox, o_gla, g1, gn, g2, wog, wgf, wgg, wfo, wgo, wo)


def _mlp_body(x_ref, g1_ref, g2_ref, wu_ref, wd_ref, y_ref, *, nchunk):
    x = x_ref[...]
    h = _rms(x, g1_ref[...]).astype(BF16)
    ff = wu_ref.shape[1] // nchunk
    acc = jnp.zeros(x.shape, F32)
    for c in range(nchunk):
        up = jnp.maximum(_dot(h, wu_ref[:, c * ff:(c + 1) * ff]), 0.0)
        acc = acc + _dot((up * up).astype(BF16), wd_ref[c * ff:(c + 1) * ff, :])
    y_ref[...] = x + _rms(acc, g2_ref[...])


def _mlp(x, g1, g2, wu, wd, tm):
    N, D = x.shape
    row = pl.BlockSpec((tm, D), lambda i: (i, 0))
    full = lambda a: pl.BlockSpec(a.shape, lambda i: (0,) * a.ndim)
    return pl.pallas_call(
        functools.partial(_mlp_body, nchunk=4),
        grid=(N // tm,),
        in_specs=[row, full(g1), full(g2), full(wu), full(wd)],
        out_specs=row,
        out_shape=jax.ShapeDtypeStruct((N, D), F32),
        compiler_params=_params(1),
        name="mlp",
    )(x, g1, g2, wu, wd)


def _rev_cumsum_lanes(x):
    n = x.shape[-1]
    lane = lax.broadcasted_iota(jnp.int32, x.shape, x.ndim - 1)
    sh = 1
    while sh < n:
        x = x + jnp.where(lane + sh < n, pltpu.roll(x, n - sh, x.ndim - 1), 0.0)
        sh *= 2
    return x


def _decode_body(pt_ref, *refs, G, NP):
    k_refs = refs[:G]
    v_refs = refs[G:2 * G]
    f_refs = refs[2 * G:3 * G]
    q_ref, kn_ref, vn_ref, cn_ref, o_ref, qs_ref, L_ref, acc_ref, car_ref, ml_ref = refs[3 * G:]
    ph = pl.program_id(1)
    g = pl.program_id(2)
    ng = pl.num_programs(2)
    H, HD, PAGE = k_refs[0].shape

    @pl.when((ph == 0) & (g == 0))
    def _():
        qs_ref[...] = jnp.broadcast_to(q_ref[...], (H, HD, PAGE))
        car_ref[...] = jnp.broadcast_to(cn_ref[...], (H, PAGE))

    @pl.when(ph == 0)
    def _():
        qs = qs_ref[...]
        carry = car_ref[...]
        for j in range(G):
            page = NP - 1 - (g * G + j)
            lf = f_refs[j][...]
            suf = _rev_cumsum_lanes(lf)
            s = jnp.sum(k_refs[j][...] * qs, axis=1)
            L_ref[page] = s + (suf - lf) + carry
            carry = carry + jnp.sum(lf, axis=1, keepdims=True)
        car_ref[...] = carry

    @pl.when((ph == 1) & (g == 0))
    def _():
        ln = jnp.sum(qs_ref[...] * kn_ref[...], axis=1)
        mx = jnp.max(jnp.max(L_ref[...], axis=0), axis=1, keepdims=True)
        m = jnp.maximum(jnp.broadcast_to(mx, (H, PAGE)), ln)
        ml_ref[0] = m
        ml_ref[1] = jnp.zeros((H, PAGE), F32)
        acc_ref[...] = jnp.zeros((H, HD, PAGE), F32)

    @pl.when(ph == 1)
    def _():
        m = ml_ref[0]
        lsum = ml_ref[1]
        acc = acc_ref[...]
        for j in range(G):
            pr = jnp.exp(L_ref[g * G + j] - m)
            lsum = lsum + pr
            acc = acc + v_refs[j][...] * pr[:, None, :]
        ml_ref[1] = lsum
        acc_ref[...] = acc

    @pl.when((ph == 1) & (g == ng - 1))
    def _():
        m = ml_ref[0]
        ln = jnp.sum(qs_ref[...] * kn_ref[...], axis=1)
        pn = jnp.exp(ln - m)[:, 0:1]
        den = jnp.sum(ml_ref[1], axis=1, keepdims=True) + pn
        num = jnp.sum(acc_ref[...], axis=2, keepdims=True) + pn[:, :, None] * vn_ref[...]
        o_ref[...] = num / den[:, :, None]


def _fox_sample(kc, vc, fc, page_table, q, kn, vn, cn, G):
    DB, NP = page_table.shape
    _, _, H, HD, PAGE = kc.shape
    ng = NP // G

    def kmap(b, ph, g, pt, j):
        return (0, pt[b, jnp.where(ph == 0, NP - 1 - (g * G + j), NP - 1 - ((ng - 1) * G + j))], 0, 0, 0)

    def vmap(b, ph, g, pt, j):
        return (0, pt[b, jnp.where(ph == 0, j, g * G + j)], 0, 0, 0)

    def fmap(b, ph, g, pt, j):
        return (0, pt[b, jnp.where(ph == 0, NP - 1 - (g * G + j), NP - 1 - ((ng - 1) * G + j))], 0, 0)

    specs = ([pl.BlockSpec((None, None, H, HD, PAGE), functools.partial(kmap, j=j)) for j in range(G)]
             + [pl.BlockSpec((None, None, H, HD, PAGE), functools.partial(vmap, j=j)) for j in range(G)]
             + [pl.BlockSpec((None, None, H, PAGE), functools.partial(fmap, j=j)) for j in range(G)]
             + [pl.BlockSpec((None, H, HD, 1), lambda b, ph, g, pt: (b, 0, 0, 0))] * 3
             + [pl.BlockSpec((None, H, 1), lambda b, ph, g, pt: (b, 0, 0))])
    return pl.pallas_call(
        functools.partial(_decode_body, G=G, NP=NP),
        grid_spec=pltpu.PrefetchScalarGridSpec(
            num_scalar_prefetch=1,
            grid=(DB, 2, ng),
            in_specs=specs,
            out_specs=pl.BlockSpec((None, H, HD, 1), lambda b, ph, g, pt: (b, 0, 0, 0)),
            scratch_shapes=[pltpu.VMEM((H, HD, PAGE), F32), pltpu.VMEM((NP, H, PAGE), F32),
                            pltpu.VMEM((H, HD, PAGE), F32), pltpu.VMEM((H, PAGE), F32),
                            pltpu.VMEM((2, H, PAGE), F32)]),
        out_shape=jax.ShapeDtypeStruct((DB, H, HD, 1), F32),
        compiler_params=_params(3),
        name="fox_sample",
    )(page_table, *([kc] * G), *([vc] * G), *([fc] * G), q, kn, vn, cn)


def _layer_weights(l, g_pre_mix, w_in, b_f, w_alpha_up, b_alpha, g_gla_norm, w_fox_out, w_gla_out, w_o,
                   g_post_mix, g_pre_mlp, w_up, w_down, g_post_mlp):
    sizes = (FOX_W, FOX_W, FOX_W, FOX_HEADS, GLA_K_W, GLA_K_W, GLA_V_W, GLA_V_W, GLA_RANK)
    off = [0]
    for s in sizes:
        off.append(off[-1] + s)
    w = w_in[l]
    D = w.shape[0]
    wb = w.astype(BF16)
    wqkv = wb[:, :off[3]]
    wgla = wb[:, off[4]:off[7]]
    wog = wb[:, off[7]:off[8]]
    wgf = wb[:, off[9]:off[9] + D]
    wgg = wb[:, off[9] + D:off[9] + 2 * D]
    wsm = jnp.concatenate([wb[:, off[3]:off[4]], wb[:, off[8]:off[9]],
                           jnp.zeros((D, LANES - FOX_HEADS - GLA_RANK), BF16)], axis=1)
    wau = jnp.concatenate([jnp.zeros((FOX_HEADS, GLA_K_W), BF16), w_alpha_up[l].astype(BF16),
                           jnp.zeros((LANES - FOX_HEADS - GLA_RANK, GLA_K_W), BF16)], axis=0)
    bfr = jnp.concatenate([b_f[l], jnp.zeros((LANES - FOX_HEADS,), F32)])[None, :]
    r = lambda a: a[l][None, :]
    return dict(g1=r(g_pre_mix), wqkv=wqkv, wgla=wgla, wsm=wsm, wau=wau, bal=r(b_alpha), bfr=bfr,
                gn=r(g_gla_norm), wog=wog, wgf=wgf, wgg=wgg, wfo=w_fox_out[l].astype(BF16),
                wgo=w_gla_out[l].astype(BF16), wo=w_o[l].astype(BF16), g2=r(g_post_mix), g3=r(g_pre_mlp),
                wu=w_up[l].astype(BF16), wd=w_down[l].astype(BF16), g4=r(g_post_mlp))


def _finish(x2d, o_fox, o_gla, W, tm):
    x1 = _mix(x2d, o_fox, o_gla, W["g1"], W["gn"], W["g2"], W["wog"], W["wgf"], W["wgg"], W["wfo"], W["wgo"],
              W["wo"], tm)
    return _mlp(x1, W["g3"], W["g4"], W["wu"], W["wd"], tm)


def kernel(x_prompt, x_sample, cache_k, cache_v, cache_logf, state_gla, page_table, g_pre_mix, w_in, b_f, w_alpha_up, b_alpha, g_gla_norm, w_fox_out, w_gla_out, w_o, g_post_mix, g_pre_mlp, w_up, w_down, g_post_mlp):
    B, S, D = x_prompt.shape
    DB = x_sample.shape[0]
    depth = w_in.shape[0]
    assert depth == 1 and x_sample.shape[1] == 1
    W = _layer_weights(0, g_pre_mix, w_in, b_f, w_alpha_up, b_alpha, g_gla_norm, w_fox_out, w_gla_out, w_o,
                       g_post_mix, g_pre_mlp, w_up, w_down, g_post_mlp)
    proj = lambda x, tm: _in_proj(x, W["g1"], W["wqkv"], W["wgla"], W["wsm"], W["wau"], W["bal"], W["bfr"], tm)

    kT, vT, kTb, qb, vb, lfT, qg, kg, vg, la = proj(x_prompt, ROW_TILE)
    o_fox = _fox_prompt(qb, kTb, vb, lfT, ATT_TILE)
    o_gla, s_p = _gla(qg, kg, vg, la, jnp.zeros((B, GLA_HEADS, GLA_DK, GLA_DV), F32), B)
    y_p = _finish(x_prompt.reshape(B * S, D), o_fox.reshape(B * S, FOX_W), o_gla.reshape(B * S, GLA_V_W), W,
                  ROW_TILE).reshape(B, S, D)
    to_tok = lambda a: jnp.transpose(a.reshape(1, B, FOX_HEADS, FOX_HD, S), (0, 1, 4, 2, 3))
    k_p, v_p = to_tok(kT), to_tok(vT)
    lf_p = jnp.transpose(lfT, (0, 2, 1))[None]

    PADT = LANES
    xs = jnp.zeros((1, PADT, D), F32).at[0, :DB].set(x_sample[:, 0])
    kT, vT, kTb, qb, vb, lfT, qg, kg, vg, la = proj(xs, PADT)
    k_s = kT[0].T[:DB].reshape(DB, FOX_HEADS, FOX_HD)
    v_s = vT[0].T[:DB].reshape(DB, FOX_HEADS, FOX_HD)
    lf_s = lfT[0].T[:DB]
    q_s = qb[0, :DB].astype(F32).reshape(DB, FOX_HEADS, FOX_HD)
    kc = jnp.transpose(cache_k, (0, 1, 3, 4, 2))
    vc = jnp.transpose(cache_v, (0, 1, 3, 4, 2))
    fc = jnp.transpose(cache_logf, (0, 1, 3, 2))
    o_fs = _fox_sample(kc, vc, fc, page_table, q_s[..., None], k_s[..., None], v_s[..., None], lf_s[..., None],
                       PAGES_PER_STEP)
    o_fox_s = jnp.zeros((PADT, FOX_W), BF16).at[:DB].set(o_fs.reshape(DB, FOX_W).astype(BF16))
    C = GLA_CHUNK
    pad_c = lambda a: jnp.zeros((DB, C, a.shape[-1]), a.dtype).at[:, 0].set(a[0, :DB])
    o_gs, s_s = _gla(pad_c(qg), pad_c(kg), pad_c(vg), pad_c(la), state_gla[0], DB // 2)
    o_gla_s = jnp.zeros((PADT, GLA_V_W), F32).at[:DB].set(o_gs[:, 0])
    y_s = _finish(xs[0], o_fox_s, o_gla_s, W, PADT)[:DB].reshape(DB, 1, D)

    return (y_p, y_s, k_p, v_p, lf_p, s_p[None],
            k_s[None, :, None], v_s[None, :, None], lf_s[None, :, None], s_s[None])
```

```python
import functools

import jax
import jax.numpy as jnp
from jax import lax
from jax.experimental import pallas as pl
from jax.experimental.pallas import tpu as pltpu

F32 = jnp.float32
BF16 = jnp.bfloat16

FOX_HEADS = 8
FOX_HD = 64
FOX_W = FOX_HEADS * FOX_HD
GLA_HEADS = 4
GLA_DK = 64
GLA_DV = 128
GLA_K_W = GLA_HEADS * GLA_DK
GLA_V_W = GLA_HEADS * GLA_DV
GLA_RANK = 16
GLA_GATE_NORM = 16.0
GLA_CHUNK = 64
EPS = 1e-6
LANES = 128
VMEM_LIMIT = 56 * 1024 * 1024
F_PARTS = 3

ROW_TILE = 512
ATT_TILE = 256
PAGES_PER_STEP = 8


def _params(n_axes, flags=None):
    return pltpu.CompilerParams(dimension_semantics=("arbitrary",) * n_axes,
                                vmem_limit_bytes=VMEM_LIMIT, flags=flags)


def _log_sigmoid(z):
    return jnp.minimum(z, 0.0) - jnp.log1p(jnp.exp(-jnp.abs(z)))


def _sigmoid(z):
    return 1.0 / (1.0 + jnp.exp(-z))


def _rms(x, g):
    return x * lax.rsqrt(jnp.mean(x * x, axis=-1, keepdims=True) + EPS) * g


def _dot(a, b):
    return jnp.dot(a, b, preferred_element_type=F32)


def _dot_nt(a, b):
    return lax.dot_general(a, b, (((1,), (1,)), ((), ())), preferred_element_type=F32)


def _split3(x):
    hi = x.astype(BF16)
    r1 = x - hi.astype(F32)
    mid = r1.astype(BF16)
    lo = (r1 - mid.astype(F32)).astype(BF16)
    return hi, mid, lo


def _in_proj_body(x_ref, g_ref, wqkv_ref, wgla_ref, wsm_ref, wau_ref, bal_ref, bf_ref, tri_ref,
                  kT_ref, vT_ref, vTb_ref, qTb_ref, kb_ref, fa_ref, lfT_ref, qg_ref, kg_ref, vg_ref, la_ref,
                  car_ref):
    @pl.when(pl.program_id(1) == 0)
    def _():
        car_ref[...] = jnp.zeros_like(car_ref)

    h = _rms(x_ref[...], g_ref[...]).astype(BF16)
    qkv = _dot(h, wqkv_ref[...])
    q = qkv[:, :FOX_W] * (FOX_HD ** -0.5)
    k = qkv[:, FOX_W:2 * FOX_W]
    v = qkv[:, 2 * FOX_W:]
    qTb_ref[...] = q.T.astype(BF16)
    kb_ref[...] = k.astype(BF16)
    kT_ref[...] = k.T
    vT = v.T
    vT_ref[...] = vT
    vTb_ref[...] = vT.astype(BF16)
    gl = _dot(h, wgla_ref[...])
    qg_ref[...] = gl[:, :GLA_K_W] * (GLA_DK ** -0.5)
    kg_ref[...] = gl[:, GLA_K_W:2 * GLA_K_W]
    vg_ref[...] = gl[:, 2 * GLA_K_W:].astype(BF16)
    sm = _dot(h, wsm_ref[...])
    lf = _log_sigmoid(sm + bf_ref[...])
    lfT_ref[...] = lf.T[:FOX_HEADS, :]
    z = _dot(sm.astype(BF16), wau_ref[...]) + bal_ref[...]
    la_ref[...] = _log_sigmoid(z) * (1.0 / GLA_GATE_NORM)
    tri = tri_ref[...]
    F = car_ref[...] + sum(_dot(tri, part) for part in _split3(lf))
    car_ref[...] = F[F.shape[0] - 1:, :]
    hi, mid, lo = _split3(F)
    lane = lax.broadcasted_iota(jnp.int32, F.shape, 1)
    zero = jnp.zeros_like(hi)
    fa_ref[...] = jnp.where(lane < FOX_HEADS, hi,
                            jnp.where(lane < 2 * FOX_HEADS, mid, jnp.where(lane < 3 * FOX_HEADS, lo, zero)))


def _in_proj(x, g, wqkv, wgla, wsm, wau, bal, bfr, tm):
    B, S, D = x.shape
    nt = S // tm
    tri = (lax.broadcasted_iota(jnp.int32, (tm, tm), 1) <= lax.broadcasted_iota(jnp.int32, (tm, tm), 0)).astype(BF16)
    row = lambda w: pl.BlockSpec((None, tm, w), lambda b, i: (b, i, 0))
    col = lambda w: pl.BlockSpec((None, w, tm), lambda b, i: (b, 0, i))
    full = lambda a: pl.BlockSpec(a.shape, lambda b, i: (0,) * a.ndim)
    sds = jax.ShapeDtypeStruct
    return pl.pallas_call(
        _in_proj_body,
        grid=(B, nt),
        in_specs=[row(D), full(g), full(wqkv), full(wgla), full(wsm), full(wau), full(bal), full(bfr), full(tri)],
        out_specs=[col(FOX_W), col(FOX_W), col(FOX_W), col(FOX_W), row(FOX_W), row(LANES), col(FOX_HEADS),
                   row(GLA_K_W), row(GLA_K_W), row(GLA_V_W), row(GLA_K_W)],
        out_shape=[sds((B, FOX_W, S), F32), sds((B, FOX_W, S), F32), sds((B, FOX_W, S), BF16),
                   sds((B, FOX_W, S), BF16), sds((B, S, FOX_W), BF16), sds((B, S, LANES), BF16),
                   sds((B, FOX_HEADS, S), F32),
                   sds((B, S, GLA_K_W), F32), sds((B, S, GLA_K_W), F32), sds((B, S, GLA_V_W), BF16),
                   sds((B, S, GLA_K_W), F32)],
        scratch_shapes=[pltpu.VMEM((1, LANES), F32)],
        compiler_params=_params(2),
        name="in_proj",
    )(x, g, wqkv, wgla, wsm, wau, bal, bfr, tri)


def _fox_body(k_ref, fa_ref, qT_ref, vT_ref, o_ref, rhs_ref, acc_ref, m_ref, l_ref, *, t):
    i = pl.program_id(1)
    npair = FOX_HEADS // 2

    @pl.when((pl.program_id(0) == 0) & (i == 0))
    def _():
        rr = lax.broadcasted_iota(jnp.int32, (LANES, 2 * t), 0)
        cc = lax.broadcasted_iota(jnp.int32, (LANES, 2 * t), 1)
        head = rr & (FOX_HEADS - 1)
        for p in range(npair):
            mine = head == jnp.where(cc < t, 2 * p, 2 * p + 1)
            neg = jnp.where(rr < F_PARTS * FOX_HEADS, jnp.where(mine, -1.0, 0.0), 0.0)
            rhs_ref[p, LANES:, :] = neg.astype(BF16)
            rhs_ref[p, :FOX_HD, t:] = jnp.zeros((FOX_HD, t), BF16)
            rhs_ref[p, FOX_HD:LANES, :t] = jnp.zeros((FOX_HD, t), BF16)

    for p in range(npair):
        rhs_ref[p, :FOX_HD, :t] = qT_ref[p * LANES:p * LANES + FOX_HD, :]
        rhs_ref[p, FOX_HD:LANES, t:] = qT_ref[p * LANES + FOX_HD:(p + 1) * LANES, :]
    m_ref[...] = jnp.full(m_ref.shape, -jnp.inf, F32)
    l_ref[...] = jnp.zeros(l_ref.shape, F32)
    acc_ref[...] = jnp.zeros(acc_ref.shape, F32)

    def block(j, masked):
        c0 = pl.multiple_of(j * t, t)
        fa = fa_ref[pl.ds(c0, t), :]
        if masked:
            key = lax.broadcasted_iota(jnp.int32, (t, 2 * t), 0)
            qry = lax.broadcasted_iota(jnp.int32, (t, 2 * t), 1)
            keep = key <= jnp.where(qry >= t, qry - t, qry)

        def scores(p):
            lhs = jnp.concatenate([k_ref[pl.ds(c0, t), p * LANES:(p + 1) * LANES], fa], axis=1)
            sT = _dot(lhs, rhs_ref[p])
            return jnp.where(keep, sT, -jnp.inf) if masked else sT

        def soft(p, sT):
            m_old = m_ref[p]
            m_new = jnp.maximum(m_old, jnp.max(sT, axis=0, keepdims=True))
            alpha = jnp.exp(m_old - m_new)
            pT = jnp.exp(sT - m_new)
            l_ref[p] = alpha * l_ref[p] + jnp.sum(pT, axis=0, keepdims=True)
            m_ref[p] = m_new
            return alpha, pT.astype(BF16)

        def pv(p, alpha, pT):
            acc_ref[p] = alpha * acc_ref[p] + _dot(vT_ref[p * LANES:(p + 1) * LANES, pl.ds(c0, t)], pT)

        sT = {0: scores(0), 1: scores(1)}
        done = {}
        for p in range(npair):
            done[p] = soft(p, sT.pop(p))
            if p + 2 < npair:
                sT[p + 2] = scores(p + 2)
            if p >= 1:
                pv(p - 1, *done.pop(p - 1))
        pv(npair - 1, *done.pop(npair - 1))

    def full_block(j, carry):
        block(j, False)
        return carry

    lax.fori_loop(0, i, full_block, 0)
    block(i, True)
    for p in range(npair):
        inv = 1.0 / l_ref[p]
        acc = acc_ref[p]
        oT = jnp.concatenate([acc[:FOX_HD, :t] * inv[:, :t], acc[FOX_HD:, t:] * inv[:, t:]], axis=0)
        o_ref[:, p * LANES:(p + 1) * LANES] = oT.T.astype(o_ref.dtype)


def _fox_prompt(kb, fa, qTb, vTb, t):
    B, S, _ = kb.shape
    npair = FOX_HEADS // 2
    return pl.pallas_call(
        functools.partial(_fox_body, t=t),
        grid=(B, S // t),
        in_specs=[pl.BlockSpec((None, S, FOX_W), lambda b, i: (b, 0, 0)),
                  pl.BlockSpec((None, S, LANES), lambda b, i: (b, 0, 0)),
                  pl.BlockSpec((None, FOX_W, t), lambda b, i: (b, 0, i)),
                  pl.BlockSpec((None, FOX_W, S), lambda b, i: (b, 0, 0))],
        out_specs=pl.BlockSpec((None, t, FOX_W), lambda b, i: (b, i, 0)),
        out_shape=jax.ShapeDtypeStruct((B, S, FOX_W), BF16),
        scratch_shapes=[pltpu.VMEM((npair, 2 * LANES, 2 * t), BF16), pltpu.VMEM((npair, LANES, 2 * t), F32),
                        pltpu.VMEM((npair, 1, 2 * t), F32), pltpu.VMEM((npair, 1, 2 * t), F32)],
        compiler_params=_params(2),
        name="fox_prompt",
    )(kb, fa, qTb, vTb)


def _gla_body(qg_ref, kg_ref, vg_ref, la_ref, s0_ref, o_ref, sN_ref, st_ref, *, bt, C):
    c = pl.program_id(1)
    KW, VW = GLA_K_W, GLA_V_W
    blk = (lax.broadcasted_iota(jnp.int32, (VW, KW), 0) // GLA_DV
           == lax.broadcasted_iota(jnp.int32, (VW, KW), 1) // GLA_DK)

    @pl.when(c == 0)
    def _():
        for b in range(bt):
            s0 = s0_ref[b]
            rows = [jnp.concatenate([s0[h] if hh == h else jnp.zeros((GLA_DK, GLA_DV), F32)
                                     for hh in range(GLA_HEADS)], axis=1) for h in range(GLA_HEADS)]
            st_ref[b] = jnp.concatenate(rows, axis=0).T

    tri = (lax.broadcasted_iota(jnp.int32, (C, C), 1) <= lax.broadcasted_iota(jnp.int32, (C, C), 0))
    tri_b = tri.astype(BF16)
    tri4 = jnp.concatenate([tri] * GLA_HEADS, axis=0)
    head_of_lane = lax.broadcasted_iota(jnp.int32, (C, KW), 1) // GLA_DK

    def one(b, _):
        bc = sum(_dot(tri_b, part) for part in _split3(la_ref[b]))
        ref = bc[C // 2 - 1:C // 2, :]
        last = bc[C - 1:C, :]
        q = qg_ref[b]
        k = kg_ref[b]
        v = vg_ref[b]
        st = st_ref[b]
        qi = (q * jnp.exp(bc)).astype(BF16)
        qt = q * jnp.exp(bc - ref)
        kt = (k * jnp.exp(ref - bc)).astype(BF16)
        kd = (k * jnp.exp(last - bc)).astype(BF16)
        qm = jnp.concatenate([jnp.where(head_of_lane == h, qt, 0.0) for h in range(GLA_HEADS)],
                             axis=0).astype(BF16)
        A = jnp.where(tri4, _dot_nt(qm, kt), 0.0).astype(BF16)
        intra = jnp.concatenate([_dot(A[h * C:(h + 1) * C], v[:, h * GLA_DV:(h + 1) * GLA_DV])
                                 for h in range(GLA_HEADS)], axis=1)
        inter = _dot_nt(qi, st.astype(BF16))
        o_ref[b] = inter + intra
        upd = _dot(v.astype(F32).T.astype(BF16), kd)
        st_ref[b] = jnp.where(blk, st * jnp.exp(last) + upd, 0.0)
        return 0

    lax.fori_loop(0, bt, one, 0)

    @pl.when(c == pl.num_programs(1) - 1)
    def _():
        for b in range(bt):
            s = st_ref[b].T
            for h in range(GLA_HEADS):
                sN_ref[b, h] = s[h * GLA_DK:(h + 1) * GLA_DK, h * GLA_DV:(h + 1) * GLA_DV]


def _gla(qg, kg, vg, la, s0, bt):
    B, S, _ = qg.shape
    C = GLA_CHUNK
    spec = lambda w: pl.BlockSpec((bt, C, w), lambda g, c: (g, c, 0))
    sspec = pl.BlockSpec((bt, GLA_HEADS, GLA_DK, GLA_DV), lambda g, c: (g, 0, 0, 0))
    return pl.pallas_call(
        functools.partial(_gla_body, bt=bt, C=C),
        grid=(B // bt, S // C),
        in_specs=[spec(GLA_K_W), spec(GLA_K_W), spec(GLA_V_W), spec(GLA_K_W), sspec],
        out_specs=[spec(GLA_V_W), sspec],
        out_shape=[jax.ShapeDtypeStruct((B, S, GLA_V_W), F32),
                   jax.ShapeDtypeStruct((B, GLA_HEADS, GLA_DK, GLA_DV), F32)],
        scratch_shapes=[pltpu.VMEM((bt, GLA_V_W, GLA_K_W), F32)],
        compiler_params=_params(2),
        name="gla_scan",
    )(qg, kg, vg, la, s0)


def _mix_body(x_ref, of_ref, og_ref, g1_ref, gn_ref, g2_ref, wog_ref, wgf_ref, wgg_ref, wfo_ref, wgo_ref, wo_ref,
              y_ref):
    x = x_ref[...]
    h = _rms(x, g1_ref[...]).astype(BF16)
    gate = _dot(h, wog_ref[...])
    o = og_ref[...]
    ys = []
    for hh in range(GLA_HEADS):
        sl = slice(hh * GLA_DV, (hh + 1) * GLA_DV)
        gt = gate[:, sl]
        ys.append(_rms(o[:, sl], gn_ref[...]) * (gt * _sigmoid(gt)))
    y_gla = jnp.concatenate(ys, axis=1).astype(BF16)
    u = (_sigmoid(_dot(h, wgf_ref[...])) * _dot(of_ref[...], wfo_ref[...])
         + _sigmoid(_dot(h, wgg_ref[...])) * _dot(y_gla, wgo_ref[...]))
    y_ref[...] = x + _rms(_dot(u.astype(BF16), wo_ref[...]), g2_ref[...])


def _mix(x, o_fox, o_gla, g1, gn, g2, wog, wgf, wgg, wfo, wgo, wo, tm):
    N, D = x.shape
    row = lambda w: pl.BlockSpec((tm, w), lambda i: (i, 0))
    full = lambda a: pl.BlockSpec(a.shape, lambda i: (0,) * a.ndim)
    return pl.pallas_call(
        _mix_body,
        grid=(N // tm,),
        in_specs=[row(D), row(FOX_W), row(GLA_V_W)] + [full(a) for a in (g1, gn, g2, wog, wgf, wgg, wfo, wgo, wo)],
        out_specs=row(D),
        out_shape=jax.ShapeDtypeStruct((N, D), F32),
        compiler_params=_params(1),
        name="mix",
    )(x, o_fox, o_gla, g1, gn, g2, wog, wgf, wgg, wfo, wgo, wo)


def _mlp_body(x_ref, g1_ref, g2_ref, wu_ref, wd_ref, y_ref, *, nchunk):
    x = x_ref[...]
    h = _rms(x, g1_ref[...]).astype(BF16)
    ff = wu_ref.shape[1] // nchunk
    acc = jnp.zeros(x.shape, F32)
    for c in range(nchunk):
        up = jnp.maximum(_dot(h, wu_ref[:, c * ff:(c + 1) * ff]), 0.0)
        acc = acc + _dot((up * up).astype(BF16), wd_ref[c * ff:(c + 1) * ff, :])
    y_ref[...] = x + _rms(acc, g2_ref[...])


def _mlp(x, g1, g2, wu, wd, tm):
    N, D = x.shape
    row = pl.BlockSpec((tm, D), lambda i: (i, 0))
    full = lambda a: pl.BlockSpec(a.shape, lambda i: (0,) * a.ndim)
    return pl.pallas_call(
        functools.partial(_mlp_body, nchunk=4),
        grid=(N // tm,),
        in_specs=[row, full(g1), full(g2), full(wu), full(wd)],
        out_specs=row,
        out_shape=jax.ShapeDtypeStruct((N, D), F32),
        compiler_params=_params(1),
        name="mlp",
    )(x, g1, g2, wu, wd)


def _rev_cumsum_lanes(x):
    n = x.shape[-1]
    lane = lax.broadcasted_iota(jnp.int32, x.shape, x.ndim - 1)
    sh = 1
    while sh < n:
        x = x + jnp.where(lane + sh < n, pltpu.roll(x, n - sh, x.ndim - 1), 0.0)
        sh *= 2
    return x


def _decode_body(pt_ref, *refs, G):
    k_refs = refs[:G]
    v_refs = refs[G:2 * G]
    f_refs = refs[2 * G:3 * G]
    q_ref, kn_ref, vn_ref, cn_ref, o_ref, qs_ref, acc_ref, car_ref, m_ref, l_ref = refs[3 * G:]
    g = pl.program_id(1)
    H, HD, PAGE = k_refs[0].shape

    @pl.when(g == 0)
    def _():
        qs = jnp.broadcast_to(q_ref[...], (H, HD, PAGE))
        qs_ref[...] = qs
        car_ref[...] = jnp.broadcast_to(cn_ref[...], (H, PAGE))
        for h in range(H):
            m_ref[h] = jnp.sum(qs[h] * kn_ref[h], axis=0, keepdims=True)
        l_ref[...] = jnp.zeros(l_ref.shape, F32)
        acc_ref[...] = jnp.zeros(acc_ref.shape, F32)

    carry = car_ref[...]
    bias = []
    for j in range(G):
        lf = f_refs[j][...]
        bias.append(_rev_cumsum_lanes(lf) - lf + carry)
        carry = carry + jnp.sum(lf, axis=1, keepdims=True)
    car_ref[...] = carry

    for h in range(H):
        qh = qs_ref[h]
        rows = [jnp.sum(k_refs[j][h] * qh, axis=0, keepdims=True) + bias[j][h:h + 1, :] for j in range(G)]
        mx = functools.reduce(jnp.maximum, rows)
        m_old = m_ref[h]
        m_new = jnp.maximum(m_old, jnp.max(mx, axis=1, keepdims=True))
        alpha = jnp.exp(m_old - m_new)
        a = acc_ref[h] * alpha
        ls = l_ref[h] * alpha
        for j in range(G):
            pr = jnp.exp(rows[j] - m_new)
            ls = ls + pr
            a = a + v_refs[j][h] * pr
        acc_ref[h] = a
        l_ref[h] = ls
        m_ref[h] = m_new

    @pl.when(g == pl.num_programs(1) - 1)
    def _():
        for h in range(H):
            ln = jnp.sum(qs_ref[h] * kn_ref[h], axis=0, keepdims=True)
            pn = jnp.exp(ln - m_ref[h])[:, 0:1]
            den = jnp.sum(l_ref[h], axis=1, keepdims=True) + pn
            num = jnp.sum(acc_ref[h], axis=1, keepdims=True) + pn * vn_ref[h]
            o_ref[h] = num / den


def _fox_sample(kc, vc, fc, page_table, q, kn, vn, cn, G):
    DB, NP = page_table.shape
    _, _, H, HD, PAGE = kc.shape
    ng = NP // G

    def kvmap(b, g, pt, j):
        return (0, pt[b, NP - 1 - (g * G + j)], 0, 0, 0)

    def fmap(b, g, pt, j):
        return (0, pt[b, NP - 1 - (g * G + j)], 0, 0)

    col = pl.BlockSpec((None, H, HD, 1), lambda b, g, pt: (b, 0, 0, 0))
    specs = ([pl.BlockSpec((None, None, H, HD, PAGE), functools.partial(kvmap, j=j)) for j in range(G)] * 2
             + [pl.BlockSpec((None, None, H, PAGE), functools.partial(fmap, j=j)) for j in range(G)]
             + [col, col, col, pl.BlockSpec((None, H, 1), lambda b, g, pt: (b, 0, 0))])
    return pl.pallas_call(
        functools.partial(_decode_body, G=G),
        grid_spec=pltpu.PrefetchScalarGridSpec(
            num_scalar_prefetch=1,
            grid=(DB, ng),
            in_specs=specs,
            out_specs=col,
            scratch_shapes=[pltpu.VMEM((H, HD, PAGE), F32), pltpu.VMEM((H, HD, PAGE), F32),
                            pltpu.VMEM((H, PAGE), F32), pltpu.VMEM((H, 1, PAGE), F32),
                            pltpu.VMEM((H, 1, PAGE), F32)]),
        out_shape=jax.ShapeDtypeStruct((DB, H, HD, 1), F32),
        compiler_params=_params(2),
        name="fox_sample",
    )(page_table, *([kc] * G), *([vc] * G), *([fc] * G), q, kn, vn, cn)


def _layer_weights(l, g_pre_mix, w_in, b_f, w_alpha_up, b_alpha, g_gla_norm, w_fox_out, w_gla_out, w_o,
                   g_post_mix, g_pre_mlp, w_up, w_down, g_post_mlp):
    sizes = (FOX_W, FOX_W, FOX_W, FOX_HEADS, GLA_K_W, GLA_K_W, GLA_V_W, GLA_V_W, GLA_RANK)
    off = [0]
    for s in sizes:
        off.append(off[-1] + s)
    w = w_in[l]
    D = w.shape[0]
    wb = w.astype(BF16)
    wqkv = wb[:, :off[3]]
    wgla = wb[:, off[4]:off[7]]
    wog = wb[:, off[7]:off[8]]
    wgf = wb[:, off[9]:off[9] + D]
    wgg = wb[:, off[9] + D:off[9] + 2 * D]
    nf = F_PARTS * FOX_HEADS
    pad = LANES - nf - GLA_RANK
    wsm = jnp.concatenate([wb[:, off[3]:off[4]]] * F_PARTS + [wb[:, off[8]:off[9]], jnp.zeros((D, pad), BF16)], axis=1)
    wau = jnp.concatenate([jnp.zeros((nf, GLA_K_W), BF16), w_alpha_up[l].astype(BF16),
                           jnp.zeros((pad, GLA_K_W), BF16)], axis=0)
    bfr = jnp.concatenate([b_f[l]] * F_PARTS + [jnp.zeros((LANES - nf,), F32)])[None, :]
    r = lambda a: a[l][None, :]
    return dict(g1=r(g_pre_mix), wqkv=wqkv, wgla=wgla, wsm=wsm, wau=wau, bal=r(b_alpha), bfr=bfr,
                gn=r(g_gla_norm), wog=wog, wgf=wgf, wgg=wgg, wfo=w_fox_out[l].astype(BF16),
                wgo=w_gla_out[l].astype(BF16), wo=w_o[l].astype(BF16), g2=r(g_post_mix), g3=r(g_pre_mlp),
                wu=w_up[l].astype(BF16), wd=w_down[l].astype(BF16), g4=r(g_post_mlp))


def _finish(x2d, o_fox, o_gla, W, tm):
    x1 = _mix(x2d, o_fox, o_gla, W["g1"], W["gn"], W["g2"], W["wog"], W["wgf"], W["wgg"], W["wfo"], W["wgo"],
              W["wo"], tm)
    return _mlp(x1, W["g3"], W["g4"], W["wu"], W["wd"], tm)


def kernel(x_prompt, x_sample, cache_k, cache_v, cache_logf, state_gla, page_table, g_pre_mix, w_in, b_f, w_alpha_up, b_alpha, g_gla_norm, w_fox_out, w_gla_out, w_o, g_post_mix, g_pre_mlp, w_up, w_down, g_post_mlp):
    B, S, D = x_prompt.shape
    DB = x_sample.shape[0]
    depth = w_in.shape[0]
    assert depth == 1 and x_sample.shape[1] == 1
    W = _layer_weights(0, g_pre_mix, w_in, b_f, w_alpha_up, b_alpha, g_gla_norm, w_fox_out, w_gla_out, w_o,
                       g_post_mix, g_pre_mlp, w_up, w_down, g_post_mlp)
    proj = lambda x, tm: _in_proj(x, W["g1"], W["wqkv"], W["wgla"], W["wsm"], W["wau"], W["bal"], W["bfr"], tm)

    kT, vT, vTb, qTb, kb, fa, lfT, qg, kg, vg, la = proj(x_prompt, ROW_TILE)
    o_fox = _fox_prompt(kb, fa, qTb, vTb, ATT_TILE)
    o_gla, s_p = _gla(qg, kg, vg, la, jnp.zeros((B, GLA_HEADS, GLA_DK, GLA_DV), F32), B)
    y_p = _finish(x_prompt.reshape(B * S, D), o_fox.reshape(B * S, FOX_W), o_gla.reshape(B * S, GLA_V_W), W,
                  ROW_TILE).reshape(B, S, D)
    to_tok = lambda a: jnp.transpose(a.reshape(1, B, FOX_HEADS, FOX_HD, S), (0, 1, 4, 2, 3))
    k_p, v_p = to_tok(kT), to_tok(vT)
    lf_p = jnp.transpose(lfT, (0, 2, 1))[None]

    PADT = LANES
    xs = jnp.zeros((1, PADT, D), F32).at[0, :DB].set(x_sample[:, 0])
    kT, vT, vTb, qTb, kb, fa, lfT, qg, kg, vg, la = proj(xs, PADT)
    k_s = kT[0].T[:DB].reshape(DB, FOX_HEADS, FOX_HD)
    v_s = vT[0].T[:DB].reshape(DB, FOX_HEADS, FOX_HD)
    lf_s = lfT[0].T[:DB]
    q_s = qTb[0].T[:DB].astype(F32).reshape(DB, FOX_HEADS, FOX_HD)
    kc = jnp.transpose(cache_k, (0, 1, 3, 4, 2))
    vc = jnp.transpose(cache_v, (0, 1, 3, 4, 2))
    fc = jnp.transpose(cache_logf, (0, 1, 3, 2))
    o_fs = _fox_sample(kc, vc, fc, page_table, q_s[..., None], k_s[..., None], v_s[..., None], lf_s[..., None],
                       PAGES_PER_STEP)
    o_fox_s = jnp.zeros((PADT, FOX_W), BF16).at[:DB].set(o_fs.reshape(DB, FOX_W).astype(BF16))
    C = GLA_CHUNK
    pad_c = lambda a: jnp.zeros((DB, C, a.shape[-1]), a.dtype).at[:, 0].set(a[0, :DB])
    o_gs, s_s = _gla(pad_c(qg), pad_c(kg), pad_c(vg), pad_c(la), state_gla[0], DB // 2)
    o_gla_s = jnp.zeros((PADT, GLA_V_W), F32).at[:DB].set(o_gs[:, 0])
    y_s = _finish(xs[0], o_fox_s, o_gla_s, W, PADT)[:DB].reshape(DB, 1, D)

    return (y_p, y_s, k_p, v_p, lf_p, s_p[None],
            k_s[None, :, None], v_s[None, :, None], lf_s[None, :, None], s_s[None])
```

```python
import functools
import math

import jax
import jax.numpy as jnp
from jax import lax
from jax.experimental import pallas as pl
from jax.experimental.pallas import tpu as pltpu

F32 = jnp.float32
BF16 = jnp.bfloat16

FOX_HEADS = 8
FOX_HD = 64
FOX_W = FOX_HEADS * FOX_HD
GLA_HEADS = 4
GLA_DK = 64
GLA_DV = 128
GLA_K_W = GLA_HEADS * GLA_DK
GLA_V_W = GLA_HEADS * GLA_DV
GLA_RANK = 16
GLA_GATE_NORM = 16.0
GLA_CHUNK = 64
EPS = 1e-6
LANES = 128
VMEM_LIMIT = 56 * 1024 * 1024
LOG2E = 1.4426950408889634
F_PARTS = 3

ROW_TILE = 512
ATT_TILE = 256
PAGES_PER_STEP = 8
GLA_INTERLEAVE = 4


def _params(n_axes, flags=None):
    return pltpu.CompilerParams(dimension_semantics=("arbitrary",) * n_axes,
                                vmem_limit_bytes=VMEM_LIMIT, flags=flags)


def _log_sigmoid(z):
    return jnp.minimum(z, 0.0) - jnp.log1p(jnp.exp(-jnp.abs(z)))


def _sigmoid(z):
    return 1.0 / (1.0 + jnp.exp(-z))


def _rms(x, g):
    return x * lax.rsqrt(jnp.mean(x * x, axis=-1, keepdims=True) + EPS) * g


def _dot(a, b):
    return jnp.dot(a, b, preferred_element_type=F32)


def _dot_nt(a, b):
    return lax.dot_general(a, b, (((1,), (1,)), ((), ())), preferred_element_type=F32)


def _split3(x):
    hi = x.astype(BF16)
    r1 = x - hi.astype(F32)
    mid = r1.astype(BF16)
    lo = (r1 - mid.astype(F32)).astype(BF16)
    return hi, mid, lo


def _in_proj_body(x_ref, g_ref, wqkv_ref, wgla_ref, wsm_ref, wau_ref, bal_ref, bf_ref, tri_ref,
                  kT_ref, vT_ref, vTb_ref, qTb_ref, kb_ref, fa_ref, lfT_ref, qg_ref, kg_ref, vg_ref, la_ref,
                  car_ref, *, logit_scale):
    @pl.when(pl.program_id(1) == 0)
    def _():
        car_ref[...] = jnp.zeros_like(car_ref)

    h = _rms(x_ref[...], g_ref[...]).astype(BF16)
    qkv = _dot(h, wqkv_ref[...])
    q = qkv[:, :FOX_W] * (FOX_HD ** -0.5 * logit_scale)
    k = qkv[:, FOX_W:2 * FOX_W]
    v = qkv[:, 2 * FOX_W:]
    qTb_ref[...] = q.T.astype(BF16)
    kb_ref[...] = k.astype(BF16)
    kT_ref[...] = k.T
    vT = v.T
    vT_ref[...] = vT
    vTb_ref[...] = vT.astype(BF16)
    gl = _dot(h, wgla_ref[...])
    qg_ref[...] = gl[:, :GLA_K_W] * (GLA_DK ** -0.5)
    kg_ref[...] = gl[:, GLA_K_W:2 * GLA_K_W]
    vg_ref[...] = gl[:, 2 * GLA_K_W:].astype(BF16)
    sm = _dot(h, wsm_ref[...])
    lf = _log_sigmoid(sm + bf_ref[...])
    lfT_ref[...] = lf.T[:FOX_HEADS, :]
    z = _dot(sm.astype(BF16), wau_ref[...]) + bal_ref[...]
    la_ref[...] = _log_sigmoid(z) * (1.0 / GLA_GATE_NORM)
    tri = tri_ref[...]
    F = car_ref[...] + sum(_dot(tri, part) for part in _split3(lf))
    car_ref[...] = F[F.shape[0] - 1:, :]
    hi, mid, lo = _split3(F * logit_scale)
    lane = lax.broadcasted_iota(jnp.int32, F.shape, 1)
    zero = jnp.zeros_like(hi)
    fa_ref[...] = jnp.where(lane < FOX_HEADS, hi,
                            jnp.where(lane < 2 * FOX_HEADS, mid, jnp.where(lane < 3 * FOX_HEADS, lo, zero)))


def _in_proj(x, g, wqkv, wgla, wsm, wau, bal, bfr, tm, logit_scale):
    B, S, D = x.shape
    nt = S // tm
    tri = (lax.broadcasted_iota(jnp.int32, (tm, tm), 1) <= lax.broadcasted_iota(jnp.int32, (tm, tm), 0)).astype(BF16)
    row = lambda w: pl.BlockSpec((None, tm, w), lambda b, i: (b, i, 0))
    col = lambda w: pl.BlockSpec((None, w, tm), lambda b, i: (b, 0, i))
    full = lambda a: pl.BlockSpec(a.shape, lambda b, i: (0,) * a.ndim)
    sds = jax.ShapeDtypeStruct
    return pl.pallas_call(
        functools.partial(_in_proj_body, logit_scale=logit_scale),
        grid=(B, nt),
        in_specs=[row(D), full(g), full(wqkv), full(wgla), full(wsm), full(wau), full(bal), full(bfr), full(tri)],
        out_specs=[col(FOX_W), col(FOX_W), col(FOX_W), col(FOX_W), row(FOX_W), row(LANES), col(FOX_HEADS),
                   row(GLA_K_W), row(GLA_K_W), row(GLA_V_W), row(GLA_K_W)],
        out_shape=[sds((B, FOX_W, S), F32), sds((B, FOX_W, S), F32), sds((B, FOX_W, S), BF16),
                   sds((B, FOX_W, S), BF16), sds((B, S, FOX_W), BF16), sds((B, S, LANES), BF16),
                   sds((B, FOX_HEADS, S), F32),
                   sds((B, S, GLA_K_W), F32), sds((B, S, GLA_K_W), F32), sds((B, S, GLA_V_W), BF16),
                   sds((B, S, GLA_K_W), F32)],
        scratch_shapes=[pltpu.VMEM((1, LANES), F32)],
        compiler_params=_params(2),
        name="in_proj",
    )(x, g, wqkv, wgla, wsm, wau, bal, bfr, tri)


def _fox_body(k_ref, fa_ref, qT_ref, vT_ref, o_ref, rhs_ref, acc_ref, m_ref, *, t):
    i = pl.program_id(1)
    npair = FOX_HEADS // 2
    ONES = 16

    @pl.when((pl.program_id(0) == 0) & (i == 0))
    def _():
        rr = lax.broadcasted_iota(jnp.int32, (LANES, 2 * t), 0)
        cc = lax.broadcasted_iota(jnp.int32, (LANES, 2 * t), 1)
        head = rr & (FOX_HEADS - 1)
        for p in range(npair):
            mine = head == jnp.where(cc < t, 2 * p, 2 * p + 1)
            neg = jnp.where(rr < F_PARTS * FOX_HEADS, jnp.where(mine, -1.0, 0.0), 0.0)
            rhs_ref[p, LANES:, :] = neg.astype(BF16)
            rhs_ref[p, :FOX_HD, t:] = jnp.zeros((FOX_HD, t), BF16)
            rhs_ref[p, FOX_HD:LANES, :t] = jnp.zeros((FOX_HD, t), BF16)

    for p in range(npair):
        rhs_ref[p, :FOX_HD, :t] = qT_ref[p * LANES:p * LANES + FOX_HD, :]
        rhs_ref[p, FOX_HD:LANES, t:] = qT_ref[p * LANES + FOX_HD:(p + 1) * LANES, :]
    m_ref[...] = jnp.full(m_ref.shape, -jnp.inf, F32)
    acc_ref[...] = jnp.zeros(acc_ref.shape, F32)
    ones = jnp.where(lax.broadcasted_iota(jnp.int32, (ONES, t), 0) == 0, 1.0, 0.0).astype(BF16)

    def block(j, masked):
        c0 = pl.multiple_of(j * t, t)
        fa = fa_ref[pl.ds(c0, t), :]
        if masked:
            key = lax.broadcasted_iota(jnp.int32, (t, 2 * t), 0)
            qry = lax.broadcasted_iota(jnp.int32, (t, 2 * t), 1)
            keep = key <= jnp.where(qry >= t, qry - t, qry)

        def scores(p):
            lhs = jnp.concatenate([k_ref[pl.ds(c0, t), p * LANES:(p + 1) * LANES], fa], axis=1)
            sT = _dot(lhs, rhs_ref[p])
            return jnp.where(keep, sT, -jnp.inf) if masked else sT

        def soft(p, sT):
            m_old = m_ref[p]
            m_new = jnp.maximum(m_old, jnp.max(sT, axis=0, keepdims=True))
            m_ref[p] = m_new
            return jnp.exp2(m_old - m_new), jnp.exp2(sT - m_new).astype(BF16)

        def pv(p, alpha, pT):
            v1 = jnp.concatenate([vT_ref[p * LANES:(p + 1) * LANES, pl.ds(c0, t)], ones], axis=0)
            acc_ref[p] = alpha * acc_ref[p] + _dot(v1, pT)

        sT = {0: scores(0), 1: scores(1)}
        done = {}
        for p in range(npair):
            done[p] = soft(p, sT.pop(p))
            if p + 2 < npair:
                sT[p + 2] = scores(p + 2)
            if p >= 1:
                pv(p - 1, *done.pop(p - 1))
        pv(npair - 1, *done.pop(npair - 1))

    def full_block(j, carry):
        block(j, False)
        return carry

    lax.fori_loop(0, i, full_block, 0)
    block(i, True)
    for p in range(npair):
        acc = acc_ref[p]
        inv = 1.0 / acc[LANES:LANES + 1, :]
        oT = jnp.concatenate([acc[:FOX_HD, :t] * inv[:, :t], acc[FOX_HD:LANES, t:] * inv[:, t:]], axis=0)
        o_ref[:, p * LANES:(p + 1) * LANES] = oT.T.astype(o_ref.dtype)


def _fox_prompt(kb, fa, qTb, vTb, t):
    B, S, _ = kb.shape
    npair = FOX_HEADS // 2
    return pl.pallas_call(
        functools.partial(_fox_body, t=t),
        grid=(B, S // t),
        in_specs=[pl.BlockSpec((None, S, FOX_W), lambda b, i: (b, 0, 0)),
                  pl.BlockSpec((None, S, LANES), lambda b, i: (b, 0, 0)),
                  pl.BlockSpec((None, FOX_W, t), lambda b, i: (b, 0, i)),
                  pl.BlockSpec((None, FOX_W, S), lambda b, i: (b, 0, 0))],
        out_specs=pl.BlockSpec((None, t, FOX_W), lambda b, i: (b, i, 0)),
        out_shape=jax.ShapeDtypeStruct((B, S, FOX_W), BF16),
        scratch_shapes=[pltpu.VMEM((npair, 2 * LANES, 2 * t), BF16), pltpu.VMEM((npair, LANES + 16, 2 * t), F32),
                        pltpu.VMEM((npair, 1, 2 * t), F32)],
        compiler_params=_params(2),
        name="fox_prompt",
    )(kb, fa, qTb, vTb)


def _gla_body(qg_ref, kg_ref, vg_ref, la_ref, s0_ref, o_ref, sN_ref, st_ref, *, bt, C):
    c = pl.program_id(1)
    KW, VW = GLA_K_W, GLA_V_W
    blk = (lax.broadcasted_iota(jnp.int32, (VW, KW), 0) // GLA_DV
           == lax.broadcasted_iota(jnp.int32, (VW, KW), 1) // GLA_DK)

    @pl.when(c == 0)
    def _():
        for b in range(bt):
            s0 = s0_ref[b]
            rows = [jnp.concatenate([s0[h] if hh == h else jnp.zeros((GLA_DK, GLA_DV), F32)
                                     for hh in range(GLA_HEADS)], axis=1) for h in range(GLA_HEADS)]
            st_ref[b] = jnp.concatenate(rows, axis=0).T

    tri = (lax.broadcasted_iota(jnp.int32, (C, C), 1) <= lax.broadcasted_iota(jnp.int32, (C, C), 0))
    tri_b = tri.astype(BF16)
    tri4 = jnp.concatenate([tri] * GLA_HEADS, axis=0)
    head_of_lane = lax.broadcasted_iota(jnp.int32, (C, KW), 1) // GLA_DK

    def chain(b):
        bc = sum(_dot(tri_b, part) for part in _split3(la_ref[b]))
        yield
        ref = bc[C // 2 - 1:C // 2, :]
        last = bc[C - 1:C, :]
        q = qg_ref[b]
        k = kg_ref[b]
        v = vg_ref[b]
        st = st_ref[b]
        qi = (q * jnp.exp(bc)).astype(BF16)
        qt = q * jnp.exp(bc - ref)
        kt = (k * jnp.exp(ref - bc)).astype(BF16)
        kd = (k * jnp.exp(last - bc)).astype(BF16)
        qm = jnp.concatenate([jnp.where(head_of_lane == h, qt, 0.0) for h in range(GLA_HEADS)],
                             axis=0).astype(BF16)
        a_raw = _dot_nt(qm, kt)
        inter = _dot_nt(qi, st.astype(BF16))
        upd = _dot(v.astype(F32).T.astype(BF16), kd)
        yield
        A = jnp.where(tri4, a_raw, 0.0).astype(BF16)
        intra = jnp.concatenate([_dot(A[h * C:(h + 1) * C], v[:, h * GLA_DV:(h + 1) * GLA_DV])
                                 for h in range(GLA_HEADS)], axis=1)
        yield
        o_ref[b] = inter + intra
        st_ref[b] = jnp.where(blk, st * jnp.exp(last) + upd, 0.0)
        yield

    width = math.gcd(bt, GLA_INTERLEAVE)

    def group(i, _):
        chains = [chain(i * width + r) for r in range(width)]
        for _stage in range(4):
            for ch in chains:
                next(ch)
        return 0

    lax.fori_loop(0, bt // width, group, 0)

    @pl.when(c == pl.num_programs(1) - 1)
    def _():
        for b in range(bt):
            s = st_ref[b].T
            for h in range(GLA_HEADS):
                sN_ref[b, h] = s[h * GLA_DK:(h + 1) * GLA_DK, h * GLA_DV:(h + 1) * GLA_DV]


def _gla(qg, kg, vg, la, s0, bt):
    B, S, _ = qg.shape
    C = GLA_CHUNK
    spec = lambda w: pl.BlockSpec((bt, C, w), lambda g, c: (g, c, 0))
    sspec = pl.BlockSpec((bt, GLA_HEADS, GLA_DK, GLA_DV), lambda g, c: (g, 0, 0, 0))
    return pl.pallas_call(
        functools.partial(_gla_body, bt=bt, C=C),
        grid=(B // bt, S // C),
        in_specs=[spec(GLA_K_W), spec(GLA_K_W), spec(GLA_V_W), spec(GLA_K_W), sspec],
        out_specs=[spec(GLA_V_W), sspec],
        out_shape=[jax.ShapeDtypeStruct((B, S, GLA_V_W), F32),
                   jax.ShapeDtypeStruct((B, GLA_HEADS, GLA_DK, GLA_DV), F32)],
        scratch_shapes=[pltpu.VMEM((bt, GLA_V_W, GLA_K_W), F32)],
        compiler_params=_params(2),
        name="gla_scan",
    )(qg, kg, vg, la, s0)


def _finish_body(x_ref, of_ref, og_ref, g1_ref, gn_ref, g2_ref, g3_ref, g4_ref, wog_ref, wgf_ref, wgg_ref, wfo_ref,
                 wgo_ref, wo_ref, wu_ref, wd_ref, y_ref, *, nchunk):
    x = x_ref[...]
    h = _rms(x, g1_ref[...]).astype(BF16)
    gate = _dot(h, wog_ref[...])
    o = og_ref[...]
    ys = []
    for hh in range(GLA_HEADS):
        sl = slice(hh * GLA_DV, (hh + 1) * GLA_DV)
        gt = gate[:, sl]
        ys.append(_rms(o[:, sl], gn_ref[...]) * (gt * _sigmoid(gt)))
    y_gla = jnp.concatenate(ys, axis=1).astype(BF16)
    u = (_sigmoid(_dot(h, wgf_ref[...])) * _dot(of_ref[...], wfo_ref[...])
         + _sigmoid(_dot(h, wgg_ref[...])) * _dot(y_gla, wgo_ref[...]))
    x1 = x + _rms(_dot(u.astype(BF16), wo_ref[...]), g2_ref[...])
    h2 = _rms(x1, g3_ref[...]).astype(BF16)
    ff = wu_ref.shape[1] // nchunk
    acc = jnp.zeros(x.shape, F32)
    for c in range(nchunk):
        up = jnp.maximum(_dot(h2, wu_ref[:, c * ff:(c + 1) * ff]), 0.0)
        acc = acc + _dot((up * up).astype(BF16), wd_ref[c * ff:(c + 1) * ff, :])
    y_ref[...] = x1 + _rms(acc, g4_ref[...])


def _finish(x, o_fox, o_gla, W, tm):
    N, D = x.shape
    row = lambda w: pl.BlockSpec((tm, w), lambda i: (i, 0))
    full = lambda a: pl.BlockSpec(a.shape, lambda i: (0,) * a.ndim, pipeline_mode=pl.Buffered(1))
    consts = [W[n] for n in ("g1", "gn", "g2", "g3", "g4", "wog", "wgf", "wgg", "wfo", "wgo", "wo", "wu", "wd")]
    return pl.pallas_call(
        functools.partial(_finish_body, nchunk=4),
        grid=(N // tm,),
        in_specs=[row(D), row(FOX_W), row(GLA_V_W)] + [full(a) for a in consts],
        out_specs=row(D),
        out_shape=jax.ShapeDtypeStruct((N, D), F32),
        compiler_params=_params(1),
        name="finish",
    )(x, o_fox, o_gla, *consts)


def _decode_body(pt_ref, *refs, G):
    k_refs = refs[:G]
    v_refs = refs[G:2 * G]
    f_refs = refs[2 * G:3 * G]
    q_ref, kn_ref, vn_ref, cn_ref, w_ref, o_ref, qs_ref, acc_ref, car_ref, m_ref, l_ref = refs[3 * G:]
    g = pl.program_id(1)
    H, HD, PAGE = k_refs[0].shape

    @pl.when(g == 0)
    def _():
        qs_ref[...] = jnp.broadcast_to(q_ref[...], (H, HD, PAGE))
        car_ref[...] = jnp.broadcast_to(cn_ref[...], (H, PAGE))
        m_ref[...] = jnp.full(m_ref.shape, -jnp.inf, F32)
        l_ref[...] = jnp.zeros(l_ref.shape, F32)
        acc_ref[...] = jnp.zeros(acc_ref.shape, F32)

    lf = jnp.concatenate([f_refs[j][...] for j in range(G)], axis=0)
    w = w_ref[...]
    suf = sum(_dot(part, w) for part in _split3(lf))
    carry = car_ref[...]
    bias = []
    for j in range(G):
        bias.append(suf[j * H:(j + 1) * H, :PAGE] + carry)
        carry = carry + suf[j * H:(j + 1) * H, PAGE:]
    car_ref[...] = carry

    for h in range(H):
        qh = qs_ref[h]
        rows = [jnp.sum(k_refs[j][h] * qh, axis=0, keepdims=True) + bias[j][h:h + 1, :] for j in range(G)]
        m_old = m_ref[h]
        m_new = jnp.maximum(m_old, functools.reduce(jnp.maximum, rows))
        alpha = jnp.exp(m_old - m_new)
        a = acc_ref[h] * alpha
        ls = l_ref[h] * alpha
        for j in range(G):
            pr = jnp.exp(rows[j] - m_new)
            ls = ls + pr
            a = a + v_refs[j][h] * pr
        acc_ref[h] = a
        l_ref[h] = ls
        m_ref[h] = m_new

    @pl.when(g == pl.num_programs(1) - 1)
    def _():
        for h in range(H):
            ln = jnp.sum(qs_ref[h] * kn_ref[h], axis=0, keepdims=True)
            m = m_ref[h]
            mf = jnp.maximum(jnp.max(m, axis=1, keepdims=True), ln)
            wl = jnp.exp(m - mf)
            pn = jnp.exp(ln - mf)[:, 0:1]
            den = jnp.sum(l_ref[h] * wl, axis=1, keepdims=True) + pn
            num = jnp.sum(acc_ref[h] * wl, axis=1, keepdims=True) + pn * vn_ref[h]
            o_ref[h] = num / den


def _fox_sample(kc, vc, fc, page_table, q, kn, vn, cn, G):
    DB, NP = page_table.shape
    _, _, H, HD, PAGE = kc.shape
    ng = NP // G
    pos = lax.broadcasted_iota(jnp.int32, (PAGE, 2 * PAGE), 0)
    lane = lax.broadcasted_iota(jnp.int32, (PAGE, 2 * PAGE), 1)
    wsuf = jnp.where(lane < PAGE, pos > lane, True).astype(BF16)

    def kvmap(b, g, pt, j):
        return (0, pt[b, NP - 1 - (g * G + j)], 0, 0, 0)

    def fmap(b, g, pt, j):
        return (0, pt[b, NP - 1 - (g * G + j)], 0, 0)

    col = pl.BlockSpec((None, H, HD, 1), lambda b, g, pt: (b, 0, 0, 0))
    specs = ([pl.BlockSpec((None, None, H, HD, PAGE), functools.partial(kvmap, j=j)) for j in range(G)] * 2
             + [pl.BlockSpec((None, None, H, PAGE), functools.partial(fmap, j=j)) for j in range(G)]
             + [col, col, col, pl.BlockSpec((None, H, 1), lambda b, g, pt: (b, 0, 0)),
                pl.BlockSpec(wsuf.shape, lambda b, g, pt: (0, 0))])
    return pl.pallas_call(
        functools.partial(_decode_body, G=G),
        grid_spec=pltpu.PrefetchScalarGridSpec(
            num_scalar_prefetch=1,
            grid=(DB, ng),
            in_specs=specs,
            out_specs=col,
            scratch_shapes=[pltpu.VMEM((H, HD, PAGE), F32), pltpu.VMEM((H, HD, PAGE), F32),
                            pltpu.VMEM((H, PAGE), F32), pltpu.VMEM((H, 1, PAGE), F32),
                            pltpu.VMEM((H, 1, PAGE), F32)]),
        out_shape=jax.ShapeDtypeStruct((DB, H, HD, 1), F32),
        compiler_params=_params(2),
        name="fox_sample",
    )(page_table, *([kc] * G), *([vc] * G), *([fc] * G), q, kn, vn, cn, wsuf)


def _layer_weights(l, g_pre_mix, w_in, b_f, w_alpha_up, b_alpha, g_gla_norm, w_fox_out, w_gla_out, w_o,
                   g_post_mix, g_pre_mlp, w_up, w_down, g_post_mlp):
    sizes = (FOX_W, FOX_W, FOX_W, FOX_HEADS, GLA_K_W, GLA_K_W, GLA_V_W, GLA_V_W, GLA_RANK)
    off = [0]
    for s in sizes:
        off.append(off[-1] + s)
    w = w_in[l]
    D = w.shape[0]
    wb = w.astype(BF16)
    wqkv = wb[:, :off[3]]
    wgla = wb[:, off[4]:off[7]]
    wog = wb[:, off[7]:off[8]]
    wgf = wb[:, off[9]:off[9] + D]
    wgg = wb[:, off[9] + D:off[9] + 2 * D]
    nf = F_PARTS * FOX_HEADS
    pad = LANES - nf - GLA_RANK
    wsm = jnp.concatenate([wb[:, off[3]:off[4]]] * F_PARTS + [wb[:, off[8]:off[9]], jnp.zeros((D, pad), BF16)], axis=1)
    wau = jnp.concatenate([jnp.zeros((nf, GLA_K_W), BF16), w_alpha_up[l].astype(BF16),
                           jnp.zeros((pad, GLA_K_W), BF16)], axis=0)
    bfr = jnp.concatenate([b_f[l]] * F_PARTS + [jnp.zeros((LANES - nf,), F32)])[None, :]
    r = lambda a: a[l][None, :]
    return dict(g1=r(g_pre_mix), wqkv=wqkv, wgla=wgla, wsm=wsm, wau=wau, bal=r(b_alpha), bfr=bfr,
                gn=r(g_gla_norm), wog=wog, wgf=wgf, wgg=wgg, wfo=w_fox_out[l].astype(BF16),
                wgo=w_gla_out[l].astype(BF16), wo=w_o[l].astype(BF16), g2=r(g_post_mix), g3=r(g_pre_mlp),
                wu=w_up[l].astype(BF16), wd=w_down[l].astype(BF16), g4=r(g_post_mlp))


def kernel(x_prompt, x_sample, cache_k, cache_v, cache_logf, state_gla, page_table, g_pre_mix, w_in, b_f, w_alpha_up, b_alpha, g_gla_norm, w_fox_out, w_gla_out, w_o, g_post_mix, g_pre_mlp, w_up, w_down, g_post_mlp):
    B, S, D = x_prompt.shape
    DB = x_sample.shape[0]
    depth = w_in.shape[0]
    assert depth == 1 and x_sample.shape[1] == 1
    W = _layer_weights(0, g_pre_mix, w_in, b_f, w_alpha_up, b_alpha, g_gla_norm, w_fox_out, w_gla_out, w_o,
                       g_post_mix, g_pre_mlp, w_up, w_down, g_post_mlp)
    proj = lambda x, tm, sc: _in_proj(x, W["g1"], W["wqkv"], W["wgla"], W["wsm"], W["wau"], W["bal"], W["bfr"], tm, sc)

    kT, vT, vTb, qTb, kb, fa, lfT, qg, kg, vg, la = proj(x_prompt, ROW_TILE, LOG2E)
    o_fox = _fox_prompt(kb, fa, qTb, vTb, ATT_TILE)
    o_gla, s_p = _gla(qg, kg, vg, la, jnp.zeros((B, GLA_HEADS, GLA_DK, GLA_DV), F32), B)
    y_p = _finish(x_prompt.reshape(B * S, D), o_fox.reshape(B * S, FOX_W), o_gla.reshape(B * S, GLA_V_W), W,
                  ROW_TILE).reshape(B, S, D)
    to_tok = lambda a: jnp.transpose(a.reshape(1, B, FOX_HEADS, FOX_HD, S), (0, 1, 4, 2, 3))
    k_p, v_p = to_tok(kT), to_tok(vT)
    lf_p = jnp.transpose(lfT, (0, 2, 1))[None]

    PADT = LANES
    xs = jnp.zeros((1, PADT, D), F32).at[0, :DB].set(x_sample[:, 0])
    kT, vT, vTb, qTb, kb, fa, lfT, qg, kg, vg, la = proj(xs, PADT, 1.0)
    k_s = kT[0].T[:DB].reshape(DB, FOX_HEADS, FOX_HD)
    v_s = vT[0].T[:DB].reshape(DB, FOX_HEADS, FOX_HD)
    lf_s = lfT[0].T[:DB]
    q_s = qTb[0].T[:DB].astype(F32).reshape(DB, FOX_HEADS, FOX_HD)
    kc = jnp.transpose(cache_k, (0, 1, 3, 4, 2))
    vc = jnp.transpose(cache_v, (0, 1, 3, 4, 2))
    fc = jnp.transpose(cache_logf, (0, 1, 3, 2))
    o_fs = _fox_sample(kc, vc, fc, page_table, q_s[..., None], k_s[..., None], v_s[..., None], lf_s[..., None],
                       PAGES_PER_STEP)
    o_fox_s = jnp.zeros((PADT, FOX_W), BF16).at[:DB].set(o_fs.reshape(DB, FOX_W).astype(BF16))
    C = GLA_CHUNK
    pad_c = lambda a: jnp.zeros((DB, C, a.shape[-1]), a.dtype).at[:, 0].set(a[0, :DB])
    o_gs, s_s = _gla(pad_c(qg), pad_c(kg), pad_c(vg), pad_c(la), state_gla[0], DB // 2)
    o_gla_s = jnp.zeros((PADT, GLA_V_W), F32).at[:DB].set(o_gs[:, 0])
    y_s = _finish(xs[0], o_fox_s, o_gla_s, W, PADT)[:DB].reshape(DB, 1, D)

    return (y_p, y_s, k_p, v_p, lf_p, s_p[None],
            k_s[None, :, None], v_s[None, :, None], lf_s[None, :, None], s_s[None])
```

```python
import functools
import math

import jax
import jax.numpy as jnp
from jax import lax
from jax.experimental import pallas as pl
from jax.experimental.pallas import tpu as pltpu

F32 = jnp.float32
BF16 = jnp.bfloat16

FOX_HEADS = 8
FOX_HD = 64
FOX_W = FOX_HEADS * FOX_HD
GLA_HEADS = 4
GLA_DK = 64
GLA_DV = 128
GLA_K_W = GLA_HEADS * GLA_DK
GLA_V_W = GLA_HEADS * GLA_DV
GLA_RANK = 16
GLA_GATE_NORM = 16.0
GLA_CHUNK = 64
EPS = 1e-6
LANES = 128
VMEM_LIMIT = 56 * 1024 * 1024
LOG2E = 1.4426950408889634
F_PARTS = 3

ROW_TILE = 512
ATT_TILE = 256
PAGES_PER_STEP = 8
DECODE_RING = 3
GLA_INTERLEAVE = 4


def _params(n_axes, flags=None):
    return pltpu.CompilerParams(dimension_semantics=("arbitrary",) * n_axes,
                                vmem_limit_bytes=VMEM_LIMIT, flags=flags)


def _log_sigmoid(z):
    return jnp.minimum(z, 0.0) - jnp.log1p(jnp.exp(-jnp.abs(z)))


def _sigmoid(z):
    return 1.0 / (1.0 + jnp.exp(-z))


def _rms(x, g):
    return x * lax.rsqrt(jnp.mean(x * x, axis=-1, keepdims=True) + EPS) * g


def _dot(a, b):
    return jnp.dot(a, b, preferred_element_type=F32)


def _dot_nt(a, b):
    return lax.dot_general(a, b, (((1,), (1,)), ((), ())), preferred_element_type=F32)


def _split3(x):
    hi = x.astype(BF16)
    r1 = x - hi.astype(F32)
    mid = r1.astype(BF16)
    lo = (r1 - mid.astype(F32)).astype(BF16)
    return hi, mid, lo


def _in_proj_body(x_ref, g_ref, wqkv_ref, wgla_ref, wsm_ref, wau_ref, bal_ref, bf_ref, tri_ref,
                  kT_ref, vT_ref, vTb_ref, qTb_ref, kb_ref, fa_ref, lfT_ref, qg_ref, kg_ref, vg_ref, la_ref,
                  car_ref, *, logit_scale):
    @pl.when(pl.program_id(1) == 0)
    def _():
        car_ref[...] = jnp.zeros_like(car_ref)

    h = _rms(x_ref[...], g_ref[...]).astype(BF16)
    qkv = _dot(h, wqkv_ref[...])
    q = qkv[:, :FOX_W] * (FOX_HD ** -0.5 * logit_scale)
    k = qkv[:, FOX_W:2 * FOX_W]
    v = qkv[:, 2 * FOX_W:]
    qTb_ref[...] = q.T.astype(BF16)
    kb_ref[...] = k.astype(BF16)
    kT_ref[...] = k.T
    vT = v.T
    vT_ref[...] = vT
    vTb_ref[...] = vT.astype(BF16)
    gl = _dot(h, wgla_ref[...])
    qg_ref[...] = gl[:, :GLA_K_W] * (GLA_DK ** -0.5)
    kg_ref[...] = gl[:, GLA_K_W:2 * GLA_K_W]
    vg_ref[...] = gl[:, 2 * GLA_K_W:].astype(BF16)
    sm = _dot(h, wsm_ref[...])
    lf = _log_sigmoid(sm + bf_ref[...])
    lfT_ref[...] = lf.T[:FOX_HEADS, :]
    z = _dot(sm.astype(BF16), wau_ref[...]) + bal_ref[...]
    la_ref[...] = _log_sigmoid(z) * (1.0 / GLA_GATE_NORM)
    tri = tri_ref[...]
    F = car_ref[...] + sum(_dot(tri, part) for part in _split3(lf))
    car_ref[...] = F[F.shape[0] - 1:, :]
    hi, mid, lo = _split3(F * logit_scale)
    lane = lax.broadcasted_iota(jnp.int32, F.shape, 1)
    zero = jnp.zeros_like(hi)
    fa_ref[...] = jnp.where(lane < FOX_HEADS, hi,
                            jnp.where(lane < 2 * FOX_HEADS, mid, jnp.where(lane < 3 * FOX_HEADS, lo, zero)))


def _in_proj(x, g, wqkv, wgla, wsm, wau, bal, bfr, tm, logit_scale):
    B, S, D = x.shape
    nt = S // tm
    tri = (lax.broadcasted_iota(jnp.int32, (tm, tm), 1) <= lax.broadcasted_iota(jnp.int32, (tm, tm), 0)).astype(BF16)
    row = lambda w: pl.BlockSpec((None, tm, w), lambda b, i: (b, i, 0))
    col = lambda w: pl.BlockSpec((None, w, tm), lambda b, i: (b, 0, i))
    full = lambda a: pl.BlockSpec(a.shape, lambda b, i: (0,) * a.ndim)
    sds = jax.ShapeDtypeStruct
    return pl.pallas_call(
        functools.partial(_in_proj_body, logit_scale=logit_scale),
        grid=(B, nt),
        in_specs=[row(D), full(g), full(wqkv), full(wgla), full(wsm), full(wau), full(bal), full(bfr), full(tri)],
        out_specs=[col(FOX_W), col(FOX_W), col(FOX_W), col(FOX_W), row(FOX_W), row(LANES), col(FOX_HEADS),
                   row(GLA_K_W), row(GLA_K_W), row(GLA_V_W), row(GLA_K_W)],
        out_shape=[sds((B, FOX_W, S), F32), sds((B, FOX_W, S), F32), sds((B, FOX_W, S), BF16),
                   sds((B, FOX_W, S), BF16), sds((B, S, FOX_W), BF16), sds((B, S, LANES), BF16),
                   sds((B, FOX_HEADS, S), F32),
                   sds((B, S, GLA_K_W), F32), sds((B, S, GLA_K_W), F32), sds((B, S, GLA_V_W), BF16),
                   sds((B, S, GLA_K_W), F32)],
        scratch_shapes=[pltpu.VMEM((1, LANES), F32)],
        compiler_params=_params(2),
        name="in_proj",
    )(x, g, wqkv, wgla, wsm, wau, bal, bfr, tri)


def _fox_body(k_ref, fa_ref, qT_ref, vT_ref, o_ref, rhs_ref, acc_ref, m_ref, *, t):
    i = pl.program_id(1)
    npair = FOX_HEADS // 2
    ONES = 16

    @pl.when((pl.program_id(0) == 0) & (i == 0))
    def _():
        rr = lax.broadcasted_iota(jnp.int32, (LANES, 2 * t), 0)
        cc = lax.broadcasted_iota(jnp.int32, (LANES, 2 * t), 1)
        head = rr & (FOX_HEADS - 1)
        for p in range(npair):
            mine = head == jnp.where(cc < t, 2 * p, 2 * p + 1)
            neg = jnp.where(rr < F_PARTS * FOX_HEADS, jnp.where(mine, -1.0, 0.0), 0.0)
            rhs_ref[p, LANES:, :] = neg.astype(BF16)
            rhs_ref[p, :FOX_HD, t:] = jnp.zeros((FOX_HD, t), BF16)
            rhs_ref[p, FOX_HD:LANES, :t] = jnp.zeros((FOX_HD, t), BF16)

    for p in range(npair):
        rhs_ref[p, :FOX_HD, :t] = qT_ref[p * LANES:p * LANES + FOX_HD, :]
        rhs_ref[p, FOX_HD:LANES, t:] = qT_ref[p * LANES + FOX_HD:(p + 1) * LANES, :]
    m_ref[...] = jnp.full(m_ref.shape, -jnp.inf, F32)
    acc_ref[...] = jnp.zeros(acc_ref.shape, F32)
    ones = jnp.where(lax.broadcasted_iota(jnp.int32, (ONES, t), 0) == 0, 1.0, 0.0).astype(BF16)

    def run(blocks):
        work = []
        for j, diagonal in blocks:
            c0 = pl.multiple_of(j * t, t)
            fa = fa_ref[pl.ds(c0, t), :]
            work += [(c0, fa, diagonal, p) for p in range(npair)]

        def scores(c0, fa, diagonal, p):
            lhs = jnp.concatenate([k_ref[pl.ds(c0, t), p * LANES:(p + 1) * LANES], fa], axis=1)
            sT = _dot(lhs, rhs_ref[p])
            if diagonal:
                key = lax.broadcasted_iota(jnp.int32, (t, 2 * t), 0)
                qry = lax.broadcasted_iota(jnp.int32, (t, 2 * t), 1)
                sT = jnp.where(key <= jnp.where(qry >= t, qry - t, qry), sT, -jnp.inf)
            return sT

        def soft(p, sT):
            m_old = m_ref[p]
            m_new = jnp.maximum(m_old, jnp.max(sT, axis=0, keepdims=True))
            m_ref[p] = m_new
            return jnp.exp2(m_old - m_new), jnp.exp2(sT - m_new).astype(BF16)

        def pv(c0, p, alpha, pT):
            v1 = jnp.concatenate([vT_ref[p * LANES:(p + 1) * LANES, pl.ds(c0, t)], ones], axis=0)
            acc_ref[p] = alpha * acc_ref[p] + _dot(v1, pT)

        n = len(work)
        sT = {x: scores(*work[x]) for x in range(2)}
        done = {}
        for x in range(n):
            done[x] = soft(work[x][3], sT.pop(x))
            if x + 2 < n:
                sT[x + 2] = scores(*work[x + 2])
            if x >= 1:
                pv(work[x - 1][0], work[x - 1][3], *done.pop(x - 1))
        pv(work[n - 1][0], work[n - 1][3], *done.pop(n - 1))

    def two_blocks(jj, carry):
        run([(2 * jj, False), (2 * jj + 1, False)])
        return carry

    lax.fori_loop(0, lax.shift_right_logical(i, 1), two_blocks, 0)

    @pl.when((i & 1) == 1)
    def _():
        run([(i - 1, False), (i, True)])

    @pl.when((i & 1) == 0)
    def _():
        run([(i, True)])

    for p in range(npair):
        acc = acc_ref[p]
        inv = 1.0 / acc[LANES:LANES + 1, :]
        oT = jnp.concatenate([acc[:FOX_HD, :t] * inv[:, :t], acc[FOX_HD:LANES, t:] * inv[:, t:]], axis=0)
        o_ref[:, p * LANES:(p + 1) * LANES] = oT.T.astype(o_ref.dtype)


def _fox_prompt(kb, fa, qTb, vTb, t):
    B, S, _ = kb.shape
    npair = FOX_HEADS // 2
    return pl.pallas_call(
        functools.partial(_fox_body, t=t),
        grid=(B, S // t),
        in_specs=[pl.BlockSpec((None, S, FOX_W), lambda b, i: (b, 0, 0)),
                  pl.BlockSpec((None, S, LANES), lambda b, i: (b, 0, 0)),
                  pl.BlockSpec((None, FOX_W, t), lambda b, i: (b, 0, i)),
                  pl.BlockSpec((None, FOX_W, S), lambda b, i: (b, 0, 0))],
        out_specs=pl.BlockSpec((None, t, FOX_W), lambda b, i: (b, i, 0)),
        out_shape=jax.ShapeDtypeStruct((B, S, FOX_W), BF16),
        scratch_shapes=[pltpu.VMEM((npair, 2 * LANES, 2 * t), BF16), pltpu.VMEM((npair, LANES + 16, 2 * t), F32),
                        pltpu.VMEM((npair, 1, 2 * t), F32)],
        compiler_params=_params(2),
        name="fox_prompt",
    )(kb, fa, qTb, vTb)


def _gla_body(qg_ref, kg_ref, vg_ref, la_ref, s0_ref, o_ref, sN_ref, st_ref, *, bt, C):
    c = pl.program_id(1)
    KW, VW = GLA_K_W, GLA_V_W
    blk = (lax.broadcasted_iota(jnp.int32, (VW, KW), 0) // GLA_DV
           == lax.broadcasted_iota(jnp.int32, (VW, KW), 1) // GLA_DK)

    @pl.when(c == 0)
    def _():
        for b in range(bt):
            s0 = s0_ref[b]
            rows = [jnp.concatenate([s0[h] if hh == h else jnp.zeros((GLA_DK, GLA_DV), F32)
                                     for hh in range(GLA_HEADS)], axis=1) for h in range(GLA_HEADS)]
            st_ref[b] = jnp.concatenate(rows, axis=0).T

    tri = (lax.broadcasted_iota(jnp.int32, (C, C), 1) <= lax.broadcasted_iota(jnp.int32, (C, C), 0))
    tri_b = tri.astype(BF16)
    tri4 = jnp.concatenate([tri] * GLA_HEADS, axis=0)
    head_of_lane = lax.broadcasted_iota(jnp.int32, (C, KW), 1) // GLA_DK

    def chain(b):
        bc = sum(_dot(tri_b, part) for part in _split3(la_ref[b]))
        yield
        ref = bc[C // 2 - 1:C // 2, :]
        last = bc[C - 1:C, :]
        q = qg_ref[b]
        k = kg_ref[b]
        v = vg_ref[b]
        st = st_ref[b]
        qi = (q * jnp.exp(bc)).astype(BF16)
        qt = q * jnp.exp(bc - ref)
        kt = (k * jnp.exp(ref - bc)).astype(BF16)
        kd = (k * jnp.exp(last - bc)).astype(BF16)
        qm = jnp.concatenate([jnp.where(head_of_lane == h, qt, 0.0) for h in range(GLA_HEADS)],
                             axis=0).astype(BF16)
        a_raw = _dot_nt(qm, kt)
        inter = _dot_nt(qi, st.astype(BF16))
        upd = _dot(v.astype(F32).T.astype(BF16), kd)
        yield
        A = jnp.where(tri4, a_raw, 0.0).astype(BF16)
        intra = jnp.concatenate([_dot(A[h * C:(h + 1) * C], v[:, h * GLA_DV:(h + 1) * GLA_DV])
                                 for h in range(GLA_HEADS)], axis=1)
        yield
        o_ref[b] = inter + intra
        st_ref[b] = jnp.where(blk, st * jnp.exp(last) + upd, 0.0)
        yield

    width = math.gcd(bt, GLA_INTERLEAVE)

    def group(i, _):
        chains = [chain(i * width + r) for r in range(width)]
        for _stage in range(4):
            for ch in chains:
                next(ch)
        return 0

    lax.fori_loop(0, bt // width, group, 0)

    @pl.when(c == pl.num_programs(1) - 1)
    def _():
        for b in range(bt):
            s = st_ref[b].T
            for h in range(GLA_HEADS):
                sN_ref[b, h] = s[h * GLA_DK:(h + 1) * GLA_DK, h * GLA_DV:(h + 1) * GLA_DV]


def _gla(qg, kg, vg, la, s0, bt):
    B, S, _ = qg.shape
    C = GLA_CHUNK
    spec = lambda w: pl.BlockSpec((bt, C, w), lambda g, c: (g, c, 0))
    sspec = pl.BlockSpec((bt, GLA_HEADS, GLA_DK, GLA_DV), lambda g, c: (g, 0, 0, 0))
    return pl.pallas_call(
        functools.partial(_gla_body, bt=bt, C=C),
        grid=(B // bt, S // C),
        in_specs=[spec(GLA_K_W), spec(GLA_K_W), spec(GLA_V_W), spec(GLA_K_W), sspec],
        out_specs=[spec(GLA_V_W), sspec],
        out_shape=[jax.ShapeDtypeStruct((B, S, GLA_V_W), F32),
                   jax.ShapeDtypeStruct((B, GLA_HEADS, GLA_DK, GLA_DV), F32)],
        scratch_shapes=[pltpu.VMEM((bt, GLA_V_W, GLA_K_W), F32)],
        compiler_params=_params(2),
        name="gla_scan",
    )(qg, kg, vg, la, s0)


def _finish_body(x_ref, of_ref, og_ref, g1_ref, gn_ref, g2_ref, g3_ref, g4_ref, wog_ref, wgf_ref, wgg_ref, wfo_ref,
                 wgo_ref, wo_ref, wu_ref, wd_ref, y_ref, *, nchunk):
    x = x_ref[...]
    h = _rms(x, g1_ref[...]).astype(BF16)
    gate = _dot(h, wog_ref[...])
    o = og_ref[...]
    ys = []
    for hh in range(GLA_HEADS):
        sl = slice(hh * GLA_DV, (hh + 1) * GLA_DV)
        gt = gate[:, sl]
        ys.append(_rms(o[:, sl], gn_ref[...]) * (gt * _sigmoid(gt)))
    y_gla = jnp.concatenate(ys, axis=1).astype(BF16)
    u = (_sigmoid(_dot(h, wgf_ref[...])) * _dot(of_ref[...], wfo_ref[...])
         + _sigmoid(_dot(h, wgg_ref[...])) * _dot(y_gla, wgo_ref[...]))
    x1 = x + _rms(_dot(u.astype(BF16), wo_ref[...]), g2_ref[...])
    h2 = _rms(x1, g3_ref[...]).astype(BF16)
    ff = wu_ref.shape[1] // nchunk
    acc = jnp.zeros(x.shape, F32)
    for c in range(nchunk):
        up = jnp.maximum(_dot(h2, wu_ref[:, c * ff:(c + 1) * ff]), 0.0)
        acc = acc + _dot((up * up).astype(BF16), wd_ref[c * ff:(c + 1) * ff, :])
    y_ref[...] = x1 + _rms(acc, g4_ref[...])


def _finish(x, o_fox, o_gla, W, tm):
    N, D = x.shape
    row = lambda w: pl.BlockSpec((tm, w), lambda i: (i, 0))
    full = lambda a: pl.BlockSpec(a.shape, lambda i: (0,) * a.ndim, pipeline_mode=pl.Buffered(1))
    consts = [W[n] for n in ("g1", "gn", "g2", "g3", "g4", "wog", "wgf", "wgg", "wfo", "wgo", "wo", "wu", "wd")]
    return pl.pallas_call(
        functools.partial(_finish_body, nchunk=4),
        grid=(N // tm,),
        in_specs=[row(D), row(FOX_W), row(GLA_V_W)] + [full(a) for a in consts],
        out_specs=row(D),
        out_shape=jax.ShapeDtypeStruct((N, D), F32),
        compiler_params=_params(1),
        name="finish",
    )(x, o_fox, o_gla, *consts)


def _decode_body(pt_ref, *refs, G, NP, RING):
    (kc_hbm, vc_hbm, fc_hbm, q_ref, kn_ref, vn_ref, cn_ref, w_ref, o_ref,
     kbuf, vbuf, fbuf, sem, qs_ref, acc_ref, car_ref, m_ref, l_ref) = refs
    g = pl.program_id(1)
    ng = pl.num_programs(1)
    step = pl.program_id(0) * ng + g
    n_steps = pl.num_programs(0) * ng
    _, _, H, HD, PAGE = kbuf.shape

    def page_copies(s, slot):
        bb = lax.div(s, ng)
        gg = lax.rem(s, ng)
        out = []
        for j in range(G):
            page = pt_ref[bb, NP - 1 - (gg * G + j)]
            out.append(pltpu.make_async_copy(kc_hbm.at[0, page], kbuf.at[slot, j], sem.at[0, slot]))
            out.append(pltpu.make_async_copy(vc_hbm.at[0, page], vbuf.at[slot, j], sem.at[1, slot]))
            out.append(pltpu.make_async_copy(fc_hbm.at[0, page], fbuf.at[slot, j], sem.at[2, slot]))
        return out

    @pl.when(step == 0)
    def _():
        for s in range(RING - 1):
            for cp in page_copies(s, s):
                cp.start()

    ahead = step + (RING - 1)

    @pl.when(ahead < n_steps)
    def _():
        for cp in page_copies(ahead, lax.rem(ahead, RING)):
            cp.start()

    slot = lax.rem(step, RING)
    for cp in page_copies(step, slot):
        cp.wait()
    k_refs = [kbuf.at[slot, j] for j in range(G)]
    v_refs = [vbuf.at[slot, j] for j in range(G)]
    f_refs = [fbuf.at[slot, j] for j in range(G)]

    @pl.when(g == 0)
    def _():
        qs_ref[...] = jnp.broadcast_to(q_ref[...], (H, HD, PAGE))
        car_ref[...] = jnp.broadcast_to(cn_ref[...], (H, PAGE))
        m_ref[...] = jnp.full(m_ref.shape, -jnp.inf, F32)
        l_ref[...] = jnp.zeros(l_ref.shape, F32)
        acc_ref[...] = jnp.zeros(acc_ref.shape, F32)

    lf = jnp.concatenate([f_refs[j][...] for j in range(G)], axis=0)
    w = w_ref[...]
    suf = sum(_dot(part, w) for part in _split3(lf))
    carry = car_ref[...]
    bias = []
    for j in range(G):
        bias.append(suf[j * H:(j + 1) * H, :PAGE] + carry)
        carry = carry + suf[j * H:(j + 1) * H, PAGE:]
    car_ref[...] = carry

    for h in range(H):
        qh = qs_ref[h]
        rows = [jnp.sum(k_refs[j][h] * qh, axis=0, keepdims=True) + bias[j][h:h + 1, :] for j in range(G)]
        m_old = m_ref[h]
        m_new = jnp.maximum(m_old, functools.reduce(jnp.maximum, rows))
        alpha = jnp.exp(m_old - m_new)
        a = acc_ref[h] * alpha
        ls = l_ref[h] * alpha
        for j in range(G):
            pr = jnp.exp(rows[j] - m_new)
            ls = ls + pr
            a = a + v_refs[j][h] * pr
        acc_ref[h] = a
        l_ref[h] = ls
        m_ref[h] = m_new

    @pl.when(g == pl.num_programs(1) - 1)
    def _():
        for h in range(H):
            ln = jnp.sum(qs_ref[h] * kn_ref[h], axis=0, keepdims=True)
            m = m_ref[h]
            mf = jnp.maximum(jnp.max(m, axis=1, keepdims=True), ln)
            wl = jnp.exp(m - mf)
            pn = jnp.exp(ln - mf)[:, 0:1]
            den = jnp.sum(l_ref[h] * wl, axis=1, keepdims=True) + pn
            num = jnp.sum(acc_ref[h] * wl, axis=1, keepdims=True) + pn * vn_ref[h]
            o_ref[h] = num / den


def _fox_sample(kc, vc, fc, page_table, q, kn, vn, cn, G):
    DB, NP = page_table.shape
    _, _, H, HD, PAGE = kc.shape
    ng = NP // G
    pos = lax.broadcasted_iota(jnp.int32, (PAGE, 2 * PAGE), 0)
    lane = lax.broadcasted_iota(jnp.int32, (PAGE, 2 * PAGE), 1)
    wsuf = jnp.where(lane < PAGE, pos > lane, True).astype(BF16)

    assert DB * ng >= DECODE_RING - 1
    hbm = pl.BlockSpec(memory_space=pl.ANY)
    col = pl.BlockSpec((None, H, HD, 1), lambda b, g, pt: (b, 0, 0, 0))
    specs = [hbm, hbm, hbm, col, col, col, pl.BlockSpec((None, H, 1), lambda b, g, pt: (b, 0, 0)),
             pl.BlockSpec(wsuf.shape, lambda b, g, pt: (0, 0))]
    return pl.pallas_call(
        functools.partial(_decode_body, G=G, NP=NP, RING=DECODE_RING),
        grid_spec=pltpu.PrefetchScalarGridSpec(
            num_scalar_prefetch=1,
            grid=(DB, ng),
            in_specs=specs,
            out_specs=col,
            scratch_shapes=[pltpu.VMEM((DECODE_RING, G, H, HD, PAGE), F32),
                            pltpu.VMEM((DECODE_RING, G, H, HD, PAGE), F32),
                            pltpu.VMEM((DECODE_RING, G, H, PAGE), F32),
                            pltpu.SemaphoreType.DMA((3, DECODE_RING)),
                            pltpu.VMEM((H, HD, PAGE), F32), pltpu.VMEM((H, HD, PAGE), F32),
                            pltpu.VMEM((H, PAGE), F32), pltpu.VMEM((H, 1, PAGE), F32),
                            pltpu.VMEM((H, 1, PAGE), F32)]),
        out_shape=jax.ShapeDtypeStruct((DB, H, HD, 1), F32),
        compiler_params=_params(2),
        name="fox_sample",
    )(page_table, kc, vc, fc, q, kn, vn, cn, wsuf)


def _layer_weights(l, g_pre_mix, w_in, b_f, w_alpha_up, b_alpha, g_gla_norm, w_fox_out, w_gla_out, w_o,
                   g_post_mix, g_pre_mlp, w_up, w_down, g_post_mlp):
    sizes = (FOX_W, FOX_W, FOX_W, FOX_HEADS, GLA_K_W, GLA_K_W, GLA_V_W, GLA_V_W, GLA_RANK)
    off = [0]
    for s in sizes:
        off.append(off[-1] + s)
    w = w_in[l]
    D = w.shape[0]
    wb = w.astype(BF16)
    wqkv = wb[:, :off[3]]
    wgla = wb[:, off[4]:off[7]]
    wog = wb[:, off[7]:off[8]]
    wgf = wb[:, off[9]:off[9] + D]
    wgg = wb[:, off[9] + D:off[9] + 2 * D]
    nf = F_PARTS * FOX_HEADS
    pad = LANES - nf - GLA_RANK
    wsm = jnp.concatenate([wb[:, off[3]:off[4]]] * F_PARTS + [wb[:, off[8]:off[9]], jnp.zeros((D, pad), BF16)], axis=1)
    wau = jnp.concatenate([jnp.zeros((nf, GLA_K_W), BF16), w_alpha_up[l].astype(BF16),
                           jnp.zeros((pad, GLA_K_W), BF16)], axis=0)
    bfr = jnp.concatenate([b_f[l]] * F_PARTS + [jnp.zeros((LANES - nf,), F32)])[None, :]
    r = lambda a: a[l][None, :]
    return dict(g1=r(g_pre_mix), wqkv=wqkv, wgla=wgla, wsm=wsm, wau=wau, bal=r(b_alpha), bfr=bfr,
                gn=r(g_gla_norm), wog=wog, wgf=wgf, wgg=wgg, wfo=w_fox_out[l].astype(BF16),
                wgo=w_gla_out[l].astype(BF16), wo=w_o[l].astype(BF16), g2=r(g_post_mix), g3=r(g_pre_mlp),
                wu=w_up[l].astype(BF16), wd=w_down[l].astype(BF16), g4=r(g_post_mlp))


def kernel(x_prompt, x_sample, cache_k, cache_v, cache_logf, state_gla, page_table, g_pre_mix, w_in, b_f, w_alpha_up, b_alpha, g_gla_norm, w_fox_out, w_gla_out, w_o, g_post_mix, g_pre_mlp, w_up, w_down, g_post_mlp):
    B, S, D = x_prompt.shape
    DB = x_sample.shape[0]
    depth = w_in.shape[0]
    assert depth == 1 and x_sample.shape[1] == 1
    W = _layer_weights(0, g_pre_mix, w_in, b_f, w_alpha_up, b_alpha, g_gla_norm, w_fox_out, w_gla_out, w_o,
                       g_post_mix, g_pre_mlp, w_up, w_down, g_post_mlp)
    proj = lambda x, tm, sc: _in_proj(x, W["g1"], W["wqkv"], W["wgla"], W["wsm"], W["wau"], W["bal"], W["bfr"], tm, sc)

    kT, vT, vTb, qTb, kb, fa, lfT, qg, kg, vg, la = proj(x_prompt, ROW_TILE, LOG2E)
    o_fox = _fox_prompt(kb, fa, qTb, vTb, ATT_TILE)
    o_gla, s_p = _gla(qg, kg, vg, la, jnp.zeros((B, GLA_HEADS, GLA_DK, GLA_DV), F32), B)
    y_p = _finish(x_prompt.reshape(B * S, D), o_fox.reshape(B * S, FOX_W), o_gla.reshape(B * S, GLA_V_W), W,
                  ROW_TILE).reshape(B, S, D)
    to_tok = lambda a: jnp.transpose(a.reshape(1, B, FOX_HEADS, FOX_HD, S), (0, 1, 4, 2, 3))
    k_p, v_p = to_tok(kT), to_tok(vT)
    lf_p = jnp.transpose(lfT, (0, 2, 1))[None]

    PADT = LANES
    xs = jnp.zeros((1, PADT, D), F32).at[0, :DB].set(x_sample[:, 0])
    kT, vT, vTb, qTb, kb, fa, lfT, qg, kg, vg, la = proj(xs, PADT, 1.0)
    k_s = kT[0].T[:DB].reshape(DB, FOX_HEADS, FOX_HD)
    v_s = vT[0].T[:DB].reshape(DB, FOX_HEADS, FOX_HD)
    lf_s = lfT[0].T[:DB]
    q_s = qTb[0].T[:DB].astype(F32).reshape(DB, FOX_HEADS, FOX_HD)
    kc = jnp.transpose(cache_k, (0, 1, 3, 4, 2))
    vc = jnp.transpose(cache_v, (0, 1, 3, 4, 2))
    fc = jnp.transpose(cache_logf, (0, 1, 3, 2))
    o_fs = _fox_sample(kc, vc, fc, page_table, q_s[..., None], k_s[..., None], v_s[..., None], lf_s[..., None],
                       PAGES_PER_STEP)
    o_fox_s = jnp.zeros((PADT, FOX_W), BF16).at[:DB].set(o_fs.reshape(DB, FOX_W).astype(BF16))
    C = GLA_CHUNK
    pad_c = lambda a: jnp.zeros((DB, C, a.shape[-1]), a.dtype).at[:, 0].set(a[0, :DB])
    o_gs, s_s = _gla(pad_c(qg), pad_c(kg), pad_c(vg), pad_c(la), state_gla[0], DB // 2)
    o_gla_s = jnp.zeros((PADT, GLA_V_W), F32).at[:DB].set(o_gs[:, 0])
    y_s = _finish(xs[0], o_fox_s, o_gla_s, W, PADT)[:DB].reshape(DB, 1, D)

    return (y_p, y_s, k_p, v_p, lf_p, s_p[None],
            k_s[None, :, None], v_s[None, :, None], lf_s[None, :, None], s_s[None])
```

```python
import functools
import math

import jax
import jax.numpy as jnp
from jax import lax
from jax.experimental import pallas as pl
from jax.experimental.pallas import tpu as pltpu

F32 = jnp.float32
BF16 = jnp.bfloat16

FOX_HEADS = 8
FOX_HD = 64
FOX_W = FOX_HEADS * FOX_HD
GLA_HEADS = 4
GLA_DK = 64
GLA_DV = 128
GLA_K_W = GLA_HEADS * GLA_DK
GLA_V_W = GLA_HEADS * GLA_DV
GLA_RANK = 16
GLA_GATE_NORM = 16.0
GLA_CHUNK = 64
EPS = 1e-6
LANES = 128
VMEM_LIMIT = 56 * 1024 * 1024
LOG2E = 1.4426950408889634
F_PARTS = 3

ROW_TILE = 512
ATT_TILE = 256
PAGES_PER_STEP = 8
DECODE_RING = 2
GLA_INTERLEAVE = 4


def _params(n_axes, flags=None):
    return pltpu.CompilerParams(dimension_semantics=("arbitrary",) * n_axes,
                                vmem_limit_bytes=VMEM_LIMIT, flags=flags)


def _log_sigmoid(z):
    return jnp.minimum(z, 0.0) - jnp.log1p(jnp.exp(-jnp.abs(z)))


def _sigmoid(z):
    return 1.0 / (1.0 + jnp.exp(-z))


def _rms(x, g):
    return x * lax.rsqrt(jnp.mean(x * x, axis=-1, keepdims=True) + EPS) * g


def _dot(a, b):
    return jnp.dot(a, b, preferred_element_type=F32)


def _dot_nt(a, b):
    return lax.dot_general(a, b, (((1,), (1,)), ((), ())), preferred_element_type=F32)


def _split3(x):
    hi = x.astype(BF16)
    r1 = x - hi.astype(F32)
    mid = r1.astype(BF16)
    lo = (r1 - mid.astype(F32)).astype(BF16)
    return hi, mid, lo


def _in_proj_body(x_ref, g_ref, wqkv_ref, wgla_ref, wsm_ref, wau_ref, bal_ref, bf_ref, tri_ref,
                  kT_ref, vT_ref, vTb_ref, qTb_ref, kb_ref, fa_ref, lfT_ref, qg_ref, kg_ref, vg_ref, la_ref,
                  car_ref, *, logit_scale):
    @pl.when(pl.program_id(1) == 0)
    def _():
        car_ref[...] = jnp.zeros_like(car_ref)

    h = _rms(x_ref[...], g_ref[...]).astype(BF16)
    qkv = _dot(h, wqkv_ref[...])
    q = qkv[:, :FOX_W] * (FOX_HD ** -0.5 * logit_scale)
    k = qkv[:, FOX_W:2 * FOX_W]
    v = qkv[:, 2 * FOX_W:]
    qTb_ref[...] = q.T.astype(BF16)
    kb_ref[...] = k.astype(BF16)
    kT_ref[...] = k.T
    vT = v.T
    vT_ref[...] = vT
    vTb_ref[...] = vT.astype(BF16)
    gl = _dot(h, wgla_ref[...])
    qg_ref[...] = gl[:, :GLA_K_W] * (GLA_DK ** -0.5)
    kg_ref[...] = gl[:, GLA_K_W:2 * GLA_K_W]
    vg_ref[...] = gl[:, 2 * GLA_K_W:].astype(BF16)
    sm = _dot(h, wsm_ref[...])
    lf = _log_sigmoid(sm + bf_ref[...])
    lfT_ref[...] = lf.T[:FOX_HEADS, :]
    z = _dot(sm.astype(BF16), wau_ref[...]) + bal_ref[...]
    la_ref[...] = _log_sigmoid(z) * (1.0 / GLA_GATE_NORM)
    tri = tri_ref[...]
    F = car_ref[...] + sum(_dot(tri, part) for part in _split3(lf))
    car_ref[...] = F[F.shape[0] - 1:, :]
    hi, mid, lo = _split3(F * logit_scale)
    lane = lax.broadcasted_iota(jnp.int32, F.shape, 1)
    zero = jnp.zeros_like(hi)
    fa_ref[...] = jnp.where(lane < FOX_HEADS, hi,
                            jnp.where(lane < 2 * FOX_HEADS, mid, jnp.where(lane < 3 * FOX_HEADS, lo, zero)))


def _in_proj(x, g, wqkv, wgla, wsm, wau, bal, bfr, tm, logit_scale):
    B, S, D = x.shape
    nt = S // tm
    tri = (lax.broadcasted_iota(jnp.int32, (tm, tm), 1) <= lax.broadcasted_iota(jnp.int32, (tm, tm), 0)).astype(BF16)
    row = lambda w: pl.BlockSpec((None, tm, w), lambda b, i: (b, i, 0))
    col = lambda w: pl.BlockSpec((None, w, tm), lambda b, i: (b, 0, i))
    full = lambda a: pl.BlockSpec(a.shape, lambda b, i: (0,) * a.ndim)
    sds = jax.ShapeDtypeStruct
    return pl.pallas_call(
        functools.partial(_in_proj_body, logit_scale=logit_scale),
        grid=(B, nt),
        in_specs=[row(D), full(g), full(wqkv), full(wgla), full(wsm), full(wau), full(bal), full(bfr), full(tri)],
        out_specs=[col(FOX_W), col(FOX_W), col(FOX_W), col(FOX_W), row(FOX_W), row(LANES), col(FOX_HEADS),
                   row(GLA_K_W), row(GLA_K_W), row(GLA_V_W), row(GLA_K_W)],
        out_shape=[sds((B, FOX_W, S), F32), sds((B, FOX_W, S), F32), sds((B, FOX_W, S), BF16),
                   sds((B, FOX_W, S), BF16), sds((B, S, FOX_W), BF16), sds((B, S, LANES), BF16),
                   sds((B, FOX_HEADS, S), F32),
                   sds((B, S, GLA_K_W), F32), sds((B, S, GLA_K_W), F32), sds((B, S, GLA_V_W), BF16),
                   sds((B, S, GLA_K_W), F32)],
        scratch_shapes=[pltpu.VMEM((1, LANES), F32)],
        compiler_params=_params(2),
        name="in_proj",
    )(x, g, wqkv, wgla, wsm, wau, bal, bfr, tri)


def _fox_body(k_ref, fa_ref, qT_ref, vT_ref, o_ref, rhs_ref, acc_ref, m_ref, *, t):
    i = pl.program_id(1)
    npair = FOX_HEADS // 2
    ONES = 16

    @pl.when((pl.program_id(0) == 0) & (i == 0))
    def _():
        rr = lax.broadcasted_iota(jnp.int32, (LANES, 2 * t), 0)
        cc = lax.broadcasted_iota(jnp.int32, (LANES, 2 * t), 1)
        head = rr & (FOX_HEADS - 1)
        for p in range(npair):
            mine = head == jnp.where(cc < t, 2 * p, 2 * p + 1)
            neg = jnp.where(rr < F_PARTS * FOX_HEADS, jnp.where(mine, -1.0, 0.0), 0.0)
            rhs_ref[p, LANES:, :] = neg.astype(BF16)
            rhs_ref[p, :FOX_HD, t:] = jnp.zeros((FOX_HD, t), BF16)
            rhs_ref[p, FOX_HD:LANES, :t] = jnp.zeros((FOX_HD, t), BF16)

    for p in range(npair):
        rhs_ref[p, :FOX_HD, :t] = qT_ref[p * LANES:p * LANES + FOX_HD, :]
        rhs_ref[p, FOX_HD:LANES, t:] = qT_ref[p * LANES + FOX_HD:(p + 1) * LANES, :]
    m_ref[...] = jnp.full(m_ref.shape, -jnp.inf, F32)
    acc_ref[...] = jnp.zeros(acc_ref.shape, F32)
    ones = jnp.where(lax.broadcasted_iota(jnp.int32, (ONES, t), 0) == 0, 1.0, 0.0).astype(BF16)

    def run(blocks):
        work = []
        for j, diagonal in blocks:
            c0 = pl.multiple_of(j * t, t)
            fa = fa_ref[pl.ds(c0, t), :]
            work += [(c0, fa, diagonal, p) for p in range(npair)]

        def scores(c0, fa, diagonal, p):
            lhs = jnp.concatenate([k_ref[pl.ds(c0, t), p * LANES:(p + 1) * LANES], fa], axis=1)
            sT = _dot(lhs, rhs_ref[p])
            if diagonal:
                key = lax.broadcasted_iota(jnp.int32, (t, 2 * t), 0)
                qry = lax.broadcasted_iota(jnp.int32, (t, 2 * t), 1)
                sT = jnp.where(key <= jnp.where(qry >= t, qry - t, qry), sT, -jnp.inf)
            return sT

        def soft(p, sT):
            m_old = m_ref[p]
            m_new = jnp.maximum(m_old, jnp.max(sT, axis=0, keepdims=True))
            m_ref[p] = m_new
            return jnp.exp2(m_old - m_new), jnp.exp2(sT - m_new).astype(BF16)

        def pv(c0, p, alpha, pT):
            v1 = jnp.concatenate([vT_ref[p * LANES:(p + 1) * LANES, pl.ds(c0, t)], ones], axis=0)
            acc_ref[p] = alpha * acc_ref[p] + _dot(v1, pT)

        n = len(work)
        sT = {x: scores(*work[x]) for x in range(2)}
        done = {}
        for x in range(n):
            done[x] = soft(work[x][3], sT.pop(x))
            if x + 2 < n:
                sT[x + 2] = scores(*work[x + 2])
            if x >= 1:
                pv(work[x - 1][0], work[x - 1][3], *done.pop(x - 1))
        pv(work[n - 1][0], work[n - 1][3], *done.pop(n - 1))

    def two_blocks(jj, carry):
        run([(2 * jj, False), (2 * jj + 1, False)])
        return carry

    lax.fori_loop(0, lax.shift_right_logical(i, 1), two_blocks, 0)

    @pl.when((i & 1) == 1)
    def _():
        run([(i - 1, False), (i, True)])

    @pl.when((i & 1) == 0)
    def _():
        run([(i, True)])

    for p in range(npair):
        acc = acc_ref[p]
        inv = 1.0 / acc[LANES:LANES + 1, :]
        oT = jnp.concatenate([acc[:FOX_HD, :t] * inv[:, :t], acc[FOX_HD:LANES, t:] * inv[:, t:]], axis=0)
        o_ref[:, p * LANES:(p + 1) * LANES] = oT.T.astype(o_ref.dtype)


def _fox_prompt(kb, fa, qTb, vTb, t):
    B, S, _ = kb.shape
    npair = FOX_HEADS // 2
    return pl.pallas_call(
        functools.partial(_fox_body, t=t),
        grid=(B, S // t),
        in_specs=[pl.BlockSpec((None, S, FOX_W), lambda b, i: (b, 0, 0)),
                  pl.BlockSpec((None, S, LANES), lambda b, i: (b, 0, 0)),
                  pl.BlockSpec((None, FOX_W, t), lambda b, i: (b, 0, i)),
                  pl.BlockSpec((None, FOX_W, S), lambda b, i: (b, 0, 0))],
        out_specs=pl.BlockSpec((None, t, FOX_W), lambda b, i: (b, i, 0)),
        out_shape=jax.ShapeDtypeStruct((B, S, FOX_W), BF16),
        scratch_shapes=[pltpu.VMEM((npair, 2 * LANES, 2 * t), BF16), pltpu.VMEM((npair, LANES + 16, 2 * t), F32),
                        pltpu.VMEM((npair, 1, 2 * t), F32)],
        compiler_params=_params(2),
        name="fox_prompt",
    )(kb, fa, qTb, vTb)


def _gla_body(qg_ref, kg_ref, vg_ref, la_ref, s0_ref, o_ref, sN_ref, st_ref, *, bt, C):
    c = pl.program_id(1)
    KW, VW = GLA_K_W, GLA_V_W
    blk = (lax.broadcasted_iota(jnp.int32, (VW, KW), 0) // GLA_DV
           == lax.broadcasted_iota(jnp.int32, (VW, KW), 1) // GLA_DK)

    @pl.when(c == 0)
    def _():
        for b in range(bt):
            s0 = s0_ref[b]
            rows = [jnp.concatenate([s0[h] if hh == h else jnp.zeros((GLA_DK, GLA_DV), F32)
                                     for hh in range(GLA_HEADS)], axis=1) for h in range(GLA_HEADS)]
            st_ref[b] = jnp.concatenate(rows, axis=0).T

    tri = (lax.broadcasted_iota(jnp.int32, (C, C), 1) <= lax.broadcasted_iota(jnp.int32, (C, C), 0))
    tri_b = tri.astype(BF16)
    tri4 = jnp.concatenate([tri] * GLA_HEADS, axis=0)
    head_of_lane = lax.broadcasted_iota(jnp.int32, (C, KW), 1) // GLA_DK

    def chain(b):
        bc = sum(_dot(tri_b, part) for part in _split3(la_ref[b]))
        yield
        ref = bc[C // 2 - 1:C // 2, :]
        last = bc[C - 1:C, :]
        q = qg_ref[b]
        k = kg_ref[b]
        v = vg_ref[b]
        st = st_ref[b]
        qi = (q * jnp.exp(bc)).astype(BF16)
        qt = q * jnp.exp(bc - ref)
        kt = (k * jnp.exp(ref - bc)).astype(BF16)
        kd = (k * jnp.exp(last - bc)).astype(BF16)
        qm = jnp.concatenate([jnp.where(head_of_lane == h, qt, 0.0) for h in range(GLA_HEADS)],
                             axis=0).astype(BF16)
        a_raw = _dot_nt(qm, kt)
        inter = _dot_nt(qi, st.astype(BF16))
        upd = _dot(v.astype(F32).T.astype(BF16), kd)
        yield
        A = jnp.where(tri4, a_raw, 0.0).astype(BF16)
        intra = jnp.concatenate([_dot(A[h * C:(h + 1) * C], v[:, h * GLA_DV:(h + 1) * GLA_DV])
                                 for h in range(GLA_HEADS)], axis=1)
        yield
        o_ref[b] = inter + intra
        st_ref[b] = jnp.where(blk, st * jnp.exp(last) + upd, 0.0)
        yield

    width = math.gcd(bt, GLA_INTERLEAVE)

    def group(i, _):
        chains = [chain(i * width + r) for r in range(width)]
        for _stage in range(4):
            for ch in chains:
                next(ch)
        return 0

    lax.fori_loop(0, bt // width, group, 0)

    @pl.when(c == pl.num_programs(1) - 1)
    def _():
        for b in range(bt):
            s = st_ref[b].T
            for h in range(GLA_HEADS):
                sN_ref[b, h] = s[h * GLA_DK:(h + 1) * GLA_DK, h * GLA_DV:(h + 1) * GLA_DV]


def _gla(qg, kg, vg, la, s0, bt):
    B, S, _ = qg.shape
    C = GLA_CHUNK
    spec = lambda w: pl.BlockSpec((bt, C, w), lambda g, c: (g, c, 0))
    sspec = pl.BlockSpec((bt, GLA_HEADS, GLA_DK, GLA_DV), lambda g, c: (g, 0, 0, 0))
    return pl.pallas_call(
        functools.partial(_gla_body, bt=bt, C=C),
        grid=(B // bt, S // C),
        in_specs=[spec(GLA_K_W), spec(GLA_K_W), spec(GLA_V_W), spec(GLA_K_W), sspec],
        out_specs=[spec(GLA_V_W), sspec],
        out_shape=[jax.ShapeDtypeStruct((B, S, GLA_V_W), F32),
                   jax.ShapeDtypeStruct((B, GLA_HEADS, GLA_DK, GLA_DV), F32)],
        scratch_shapes=[pltpu.VMEM((bt, GLA_V_W, GLA_K_W), F32)],
        compiler_params=_params(2),
        name="gla_scan",
    )(qg, kg, vg, la, s0)


FINISH_STAGES = 8
MLP_CHUNKS = 4


def _finish_stages(x_ref, of_ref, og_ref, g1_ref, gn_ref, g2_ref, g3_ref, g4_ref, wog_ref, wgf_ref, wgg_ref, wfo_ref,
                   wgo_ref, wo_ref, wu_ref, wd_ref, y_ref):
    x = x_ref[...]
    h = _rms(x, g1_ref[...]).astype(BF16)
    gate = _dot(h, wog_ref[...])
    gf = _dot(h, wgf_ref[...])
    yield
    gg = _dot(h, wgg_ref[...])
    fo = _dot(of_ref[...], wfo_ref[...])
    o = og_ref[...]
    ys = []
    for hh in range(GLA_HEADS):
        sl = slice(hh * GLA_DV, (hh + 1) * GLA_DV)
        gt = gate[:, sl]
        ys.append(_rms(o[:, sl], gn_ref[...]) * (gt * _sigmoid(gt)))
    y_gla = jnp.concatenate(ys, axis=1).astype(BF16)
    yield
    u = _sigmoid(gf) * fo + _sigmoid(gg) * _dot(y_gla, wgo_ref[...])
    x1 = x + _rms(_dot(u.astype(BF16), wo_ref[...]), g2_ref[...])
    h2 = _rms(x1, g3_ref[...]).astype(BF16)
    yield
    ff = wu_ref.shape[1] // MLP_CHUNKS
    acc = jnp.zeros(x.shape, F32)
    for c in range(MLP_CHUNKS):
        up = jnp.maximum(_dot(h2, wu_ref[:, c * ff:(c + 1) * ff]), 0.0)
        if c == MLP_CHUNKS - 1:
            yield
        acc = acc + _dot((up * up).astype(BF16), wd_ref[c * ff:(c + 1) * ff, :])
        if c < MLP_CHUNKS - 1:
            yield
    y_ref[...] = x1 + _rms(acc, g4_ref[...])
    yield


def _finish_body(*refs):
    for _ in _finish_stages(*refs):
        pass


def _finish_decode_body(pt_ref, *refs, G, NP, RING):
    fin, dec = refs[:16], refs[16:24]
    y_ref, o_ref = refs[24:26]
    scratch = refs[26:]
    i = pl.program_id(0)
    n_sub = pl.num_programs(0) * FINISH_STAGES
    stages = _finish_stages(*fin, y_ref)
    for r in range(FINISH_STAGES):
        next(stages)
        _decode_step(pt_ref, dec, o_ref, scratch, i * FINISH_STAGES + r, n_sub, r, FINISH_STAGES, G, NP, RING)


def _finish(x, o_fox, o_gla, W, tm):
    N, D = x.shape
    row = lambda w: pl.BlockSpec((tm, w), lambda i: (i, 0))
    full = lambda a: pl.BlockSpec(a.shape, lambda i: (0,) * a.ndim, pipeline_mode=pl.Buffered(1))
    consts = [W[n] for n in ("g1", "gn", "g2", "g3", "g4", "wog", "wgf", "wgg", "wfo", "wgo", "wo", "wu", "wd")]
    return pl.pallas_call(
        _finish_body,
        grid=(N // tm,),
        in_specs=[row(D), row(FOX_W), row(GLA_V_W)] + [full(a) for a in consts],
        out_specs=row(D),
        out_shape=jax.ShapeDtypeStruct((N, D), F32),
        compiler_params=_params(1),
        name="finish",
    )(x, o_fox, o_gla, *consts)


def _decode_step(pt_ref, dec, o_ref, scratch, step, n_steps, r, period, G, NP, RING):
    kc_hbm, vc_hbm, fc_hbm, q_ref, kn_ref, vn_ref, cn_ref, w_ref = dec
    kbuf, vbuf, fbuf, sem, qs_ref, acc_ref, car_ref, m_ref, l_ref = scratch
    ng = NP // G
    g = lax.rem(step, jnp.int32(ng))
    _, _, H, HD, PAGE = kbuf.shape
    reachable = lambda gval: (gval - r) % math.gcd(ng, period) == 0

    def page_copies(s, slot):
        bb, gg = (s // ng, s % ng) if isinstance(s, int) else (lax.div(s, jnp.int32(ng)), lax.rem(s, jnp.int32(ng)))
        out = []
        for j in range(G):
            page = pt_ref[bb, NP - 1 - (gg * G + j)]
            out.append(pltpu.make_async_copy(kc_hbm.at[0, page], kbuf.at[slot, j], sem.at[0, slot]))
            out.append(pltpu.make_async_copy(vc_hbm.at[0, page], vbuf.at[slot, j], sem.at[1, slot]))
            out.append(pltpu.make_async_copy(fc_hbm.at[0, page], fbuf.at[slot, j], sem.at[2, slot]))
        return out

    if r == 0:
        @pl.when(step == 0)
        def _():
            for s in range(RING - 1):
                for cp in page_copies(s, s):
                    cp.start()

    ahead = step + (RING - 1)

    @pl.when(ahead < n_steps)
    def _():
        for cp in page_copies(ahead, lax.rem(ahead, jnp.int32(RING))):
            cp.start()

    slot = lax.rem(step, jnp.int32(RING))
    for cp in page_copies(step, slot):
        cp.wait()
    k_refs = [kbuf.at[slot, j] for j in range(G)]
    v_refs = [vbuf.at[slot, j] for j in range(G)]
    f_refs = [fbuf.at[slot, j] for j in range(G)]

    if reachable(0):
        @pl.when(g == 0)
        def _():
            qs_ref[...] = jnp.broadcast_to(q_ref[...], (H, HD, PAGE))
            car_ref[...] = jnp.broadcast_to(cn_ref[...], (H, PAGE))
            m_ref[...] = jnp.full(m_ref.shape, -jnp.inf, F32)
            l_ref[...] = jnp.zeros(l_ref.shape, F32)
            acc_ref[...] = jnp.zeros(acc_ref.shape, F32)

    lf = jnp.concatenate([f_refs[j][...] for j in range(G)], axis=0)
    w = w_ref[...]
    suf = sum(_dot(part, w) for part in _split3(lf))
    carry = car_ref[...]
    bias = []
    for j in range(G):
        bias.append(suf[j * H:(j + 1) * H, :PAGE] + carry)
        carry = carry + suf[j * H:(j + 1) * H, PAGE:]
    car_ref[...] = carry

    for h in range(H):
        qh = qs_ref[h]
        rows = [jnp.sum(k_refs[j][h] * qh, axis=0, keepdims=True) + bias[j][h:h + 1, :] for j in range(G)]
        m_old = m_ref[h]
        m_new = jnp.maximum(m_old, functools.reduce(jnp.maximum, rows))
        alpha = jnp.exp(m_old - m_new)
        a = acc_ref[h] * alpha
        ls = l_ref[h] * alpha
        for j in range(G):
            pr = jnp.exp(rows[j] - m_new)
            ls = ls + pr
            a = a + v_refs[j][h] * pr
        acc_ref[h] = a
        l_ref[h] = ls
        m_ref[h] = m_new

    if reachable(ng - 1):
        @pl.when(g == ng - 1)
        def _():
            for h in range(H):
                ln = jnp.sum(qs_ref[h] * kn_ref[h], axis=0, keepdims=True)
                m = m_ref[h]
                mf = jnp.maximum(jnp.max(m, axis=1, keepdims=True), ln)
                wl = jnp.exp(m - mf)
                pn = jnp.exp(ln - mf)[:, 0:1]
                den = jnp.sum(l_ref[h] * wl, axis=1, keepdims=True) + pn
                num = jnp.sum(acc_ref[h] * wl, axis=1, keepdims=True) + pn * vn_ref[h]
                o_ref[h] = num / den


def _finish_decode(x, o_fox, o_gla, W, tm, kc, vc, fc, page_table, q, kn, vn, cn, G):
    N, D = x.shape
    DB, NP = page_table.shape
    _, _, H, HD, PAGE = kc.shape
    ng = NP // G
    n_tiles = N // tm
    assert ng % FINISH_STAGES == 0 and n_tiles * FINISH_STAGES == DB * ng
    tiles_per_seq = ng // FINISH_STAGES
    pos = lax.broadcasted_iota(jnp.int32, (PAGE, 2 * PAGE), 0)
    lane = lax.broadcasted_iota(jnp.int32, (PAGE, 2 * PAGE), 1)
    wsuf = jnp.where(lane < PAGE, pos > lane, True).astype(BF16)

    row = lambda w: pl.BlockSpec((tm, w), lambda i, pt: (i, 0))
    full = lambda a: pl.BlockSpec(a.shape, lambda i, pt: (0,) * a.ndim, pipeline_mode=pl.Buffered(1))
    consts = [W[n] for n in ("g1", "gn", "g2", "g3", "g4", "wog", "wgf", "wgg", "wfo", "wgo", "wo", "wu", "wd")]
    hbm = pl.BlockSpec(memory_space=pl.ANY)
    col = pl.BlockSpec((None, H, HD, 1), lambda i, pt: (i // tiles_per_seq, 0, 0, 0))
    specs = ([row(D), row(FOX_W), row(GLA_V_W)] + [full(a) for a in consts]
             + [hbm, hbm, hbm, col, col, col, pl.BlockSpec((None, H, 1), lambda i, pt: (i // tiles_per_seq, 0, 0)),
                full(wsuf)])
    return pl.pallas_call(
        functools.partial(_finish_decode_body, G=G, NP=NP, RING=DECODE_RING),
        grid_spec=pltpu.PrefetchScalarGridSpec(
            num_scalar_prefetch=1,
            grid=(n_tiles,),
            in_specs=specs,
            out_specs=[row(D), col],
            scratch_shapes=[pltpu.VMEM((DECODE_RING, G, H, HD, PAGE), F32),
                            pltpu.VMEM((DECODE_RING, G, H, HD, PAGE), F32),
                            pltpu.VMEM((DECODE_RING, G, H, PAGE), F32),
                            pltpu.SemaphoreType.DMA((3, DECODE_RING)),
                            pltpu.VMEM((H, HD, PAGE), F32), pltpu.VMEM((H, HD, PAGE), F32),
                            pltpu.VMEM((H, PAGE), F32), pltpu.VMEM((H, 1, PAGE), F32),
                            pltpu.VMEM((H, 1, PAGE), F32)]),
        out_shape=[jax.ShapeDtypeStruct((N, D), F32), jax.ShapeDtypeStruct((DB, H, HD, 1), F32)],
        compiler_params=_params(1),
        name="finish_decode",
    )(page_table, x, o_fox, o_gla, *consts, kc, vc, fc, q, kn, vn, cn, wsuf)


def _layer_weights(l, g_pre_mix, w_in, b_f, w_alpha_up, b_alpha, g_gla_norm, w_fox_out, w_gla_out, w_o,
                   g_post_mix, g_pre_mlp, w_up, w_down, g_post_mlp):
    sizes = (FOX_W, FOX_W, FOX_W, FOX_HEADS, GLA_K_W, GLA_K_W, GLA_V_W, GLA_V_W, GLA_RANK)
    off = [0]
    for s in sizes:
        off.append(off[-1] + s)
    w = w_in[l]
    D = w.shape[0]
    wb = w.astype(BF16)
    wqkv = wb[:, :off[3]]
    wgla = wb[:, off[4]:off[7]]
    wog = wb[:, off[7]:off[8]]
    wgf = wb[:, off[9]:off[9] + D]
    wgg = wb[:, off[9] + D:off[9] + 2 * D]
    nf = F_PARTS * FOX_HEADS
    pad = LANES - nf - GLA_RANK
    wsm = jnp.concatenate([wb[:, off[3]:off[4]]] * F_PARTS + [wb[:, off[8]:off[9]], jnp.zeros((D, pad), BF16)], axis=1)
    wau = jnp.concatenate([jnp.zeros((nf, GLA_K_W), BF16), w_alpha_up[l].astype(BF16),
                           jnp.zeros((pad, GLA_K_W), BF16)], axis=0)
    bfr = jnp.concatenate([b_f[l]] * F_PARTS + [jnp.zeros((LANES - nf,), F32)])[None, :]
    r = lambda a: a[l][None, :]
    return dict(g1=r(g_pre_mix), wqkv=wqkv, wgla=wgla, wsm=wsm, wau=wau, bal=r(b_alpha), bfr=bfr,
                gn=r(g_gla_norm), wog=wog, wgf=wgf, wgg=wgg, wfo=w_fox_out[l].astype(BF16),
                wgo=w_gla_out[l].astype(BF16), wo=w_o[l].astype(BF16), g2=r(g_post_mix), g3=r(g_pre_mlp),
                wu=w_up[l].astype(BF16), wd=w_down[l].astype(BF16), g4=r(g_post_mlp))


def kernel(x_prompt, x_sample, cache_k, cache_v, cache_logf, state_gla, page_table, g_pre_mix, w_in, b_f, w_alpha_up, b_alpha, g_gla_norm, w_fox_out, w_gla_out, w_o, g_post_mix, g_pre_mlp, w_up, w_down, g_post_mlp):
    B, S, D = x_prompt.shape
    DB = x_sample.shape[0]
    depth = w_in.shape[0]
    assert depth == 1 and x_sample.shape[1] == 1
    W = _layer_weights(0, g_pre_mix, w_in, b_f, w_alpha_up, b_alpha, g_gla_norm, w_fox_out, w_gla_out, w_o,
                       g_post_mix, g_pre_mlp, w_up, w_down, g_post_mlp)
    proj = lambda x, tm, sc: _in_proj(x, W["g1"], W["wqkv"], W["wgla"], W["wsm"], W["wau"], W["bal"], W["bfr"], tm, sc)

    PADT = LANES
    xs = jnp.zeros((1, PADT, D), F32).at[0, :DB].set(x_sample[:, 0])
    kT, vT, _, qTb, _, _, lfT, qg_s, kg_s, vg_s, la_s = proj(xs, PADT, 1.0)
    k_s = kT[0].T[:DB].reshape(DB, FOX_HEADS, FOX_HD)
    v_s = vT[0].T[:DB].reshape(DB, FOX_HEADS, FOX_HD)
    lf_s = lfT[0].T[:DB]
    q_s = qTb[0].T[:DB].astype(F32).reshape(DB, FOX_HEADS, FOX_HD)
    kc = jnp.transpose(cache_k, (0, 1, 3, 4, 2))
    vc = jnp.transpose(cache_v, (0, 1, 3, 4, 2))
    fc = jnp.transpose(cache_logf, (0, 1, 3, 2))

    kT, vT, vTb, qTb, kb, fa, lfT, qg, kg, vg, la = proj(x_prompt, ROW_TILE, LOG2E)
    o_fox = _fox_prompt(kb, fa, qTb, vTb, ATT_TILE)
    o_gla, s_p = _gla(qg, kg, vg, la, jnp.zeros((B, GLA_HEADS, GLA_DK, GLA_DV), F32), B)
    y_p, o_fs = _finish_decode(x_prompt.reshape(B * S, D), o_fox.reshape(B * S, FOX_W), o_gla.reshape(B * S, GLA_V_W),
                               W, ROW_TILE, kc, vc, fc, page_table, q_s[..., None], k_s[..., None], v_s[..., None],
                               lf_s[..., None], PAGES_PER_STEP)
    y_p = y_p.reshape(B, S, D)
    to_tok = lambda a: jnp.transpose(a.reshape(1, B, FOX_HEADS, FOX_HD, S), (0, 1, 4, 2, 3))
    k_p, v_p = to_tok(kT), to_tok(vT)
    lf_p = jnp.transpose(lfT, (0, 2, 1))[None]

    o_fox_s = jnp.zeros((PADT, FOX_W), BF16).at[:DB].set(o_fs.reshape(DB, FOX_W).astype(BF16))
    C = GLA_CHUNK
    pad_c = lambda a: jnp.zeros((DB, C, a.shape[-1]), a.dtype).at[:, 0].set(a[0, :DB])
    o_gs, s_s = _gla(pad_c(qg_s), pad_c(kg_s), pad_c(vg_s), pad_c(la_s), state_gla[0], DB // 2)
    o_gla_s = jnp.zeros((PADT, GLA_V_W), F32).at[:DB].set(o_gs[:, 0])
    y_s = _finish(xs[0], o_fox_s, o_gla_s, W, PADT)[:DB].reshape(DB, 1, D)

    return (y_p, y_s, k_p, v_p, lf_p, s_p[None],
            k_s[None, :, None], v_s[None, :, None], lf_s[None, :, None], s_s[None])
```

```python
import functools
import math

import jax
import jax.numpy as jnp
from jax import lax
from jax.experimental import pallas as pl
from jax.experimental.pallas import tpu as pltpu

F32 = jnp.float32
BF16 = jnp.bfloat16

FOX_HEADS = 8
FOX_HD = 64
FOX_W = FOX_HEADS * FOX_HD
GLA_HEADS = 4
GLA_DK = 64
GLA_DV = 128
GLA_K_W = GLA_HEADS * GLA_DK
GLA_V_W = GLA_HEADS * GLA_DV
GLA_RANK = 16
GLA_GATE_NORM = 16.0
GLA_CHUNK = 64
EPS = 1e-6
LANES = 128
VMEM_LIMIT = 56 * 1024 * 1024
LOG2E = 1.4426950408889634
F_PARTS = 3

ROW_TILE = 512
ATT_TILE = 256
PAGES_PER_STEP = 8
DECODE_RING = 2
CUM_BLOCK = 128
GLA_INTERLEAVE = 8


def _params(n_axes, flags=None):
    return pltpu.CompilerParams(dimension_semantics=("arbitrary",) * n_axes,
                                vmem_limit_bytes=VMEM_LIMIT, flags=flags)


def _log_sigmoid(z):
    return jnp.minimum(z, 0.0) - jnp.log1p(jnp.exp(-jnp.abs(z)))


def _sigmoid(z):
    return 1.0 / (1.0 + jnp.exp(-z))


def _rms(x, g):
    return x * lax.rsqrt(jnp.mean(x * x, axis=-1, keepdims=True) + EPS) * g


def _dot(a, b):
    return jnp.dot(a, b, preferred_element_type=F32)


def _dot_nt(a, b):
    return lax.dot_general(a, b, (((1,), (1,)), ((), ())), preferred_element_type=F32)


def _split3(x):
    hi = x.astype(BF16)
    r1 = x - hi.astype(F32)
    mid = r1.astype(BF16)
    lo = (r1 - mid.astype(F32)).astype(BF16)
    return hi, mid, lo


def _in_proj_body(x_ref, g_ref, wqkv_ref, wgla_ref, wsm_ref, wau_ref, bal_ref, bf_ref, tri_ref,
                  kT_ref, vT_ref, vTb_ref, qTb_ref, kb_ref, fa_ref, lfT_ref, qg_ref, kg_ref, vg_ref, la_ref,
                  car_ref, *, logit_scale):
    @pl.when(pl.program_id(1) == 0)
    def _():
        car_ref[...] = jnp.zeros_like(car_ref)

    h = _rms(x_ref[...], g_ref[...]).astype(BF16)
    qkv = _dot(h, wqkv_ref[...])
    q = qkv[:, :FOX_W] * (FOX_HD ** -0.5 * logit_scale)
    k = qkv[:, FOX_W:2 * FOX_W]
    v = qkv[:, 2 * FOX_W:]
    qTb_ref[...] = q.T.astype(BF16)
    kb_ref[...] = k.astype(BF16)
    kT_ref[...] = k.T
    vT = v.T
    vT_ref[...] = vT
    vTb_ref[...] = vT.astype(BF16)
    gl = _dot(h, wgla_ref[...])
    qg_ref[...] = gl[:, :GLA_K_W] * (GLA_DK ** -0.5)
    kg_ref[...] = gl[:, GLA_K_W:2 * GLA_K_W]
    vg_ref[...] = gl[:, 2 * GLA_K_W:].astype(BF16)
    sm = _dot(h, wsm_ref[...])
    lf = _log_sigmoid(sm + bf_ref[...])
    lfT_ref[...] = lf.T[:FOX_HEADS, :]
    z = _dot(sm.astype(BF16), wau_ref[...]) + bal_ref[...]
    la_ref[...] = _log_sigmoid(z) * (1.0 / GLA_GATE_NORM)
    tri = tri_ref[...]
    parts = _split3(lf)
    run = car_ref[...]
    blocks = []
    for r0 in range(0, lf.shape[0], CUM_BLOCK):
        blocks.append(run + sum(_dot(tri, part[r0:r0 + CUM_BLOCK]) for part in parts))
        run = blocks[-1][CUM_BLOCK - 1:, :]
    F = jnp.concatenate(blocks, axis=0)
    car_ref[...] = run
    hi, mid, lo = _split3(F * logit_scale)
    lane = lax.broadcasted_iota(jnp.int32, F.shape, 1)
    zero = jnp.zeros_like(hi)
    fa_ref[...] = jnp.where(lane < FOX_HEADS, hi,
                            jnp.where(lane < 2 * FOX_HEADS, mid, jnp.where(lane < 3 * FOX_HEADS, lo, zero)))


def _in_proj(x, g, wqkv, wgla, wsm, wau, bal, bfr, tm, logit_scale):
    B, S, D = x.shape
    nt = S // tm
    assert tm % CUM_BLOCK == 0
    tri = (lax.broadcasted_iota(jnp.int32, (CUM_BLOCK, CUM_BLOCK), 1)
           <= lax.broadcasted_iota(jnp.int32, (CUM_BLOCK, CUM_BLOCK), 0)).astype(BF16)
    row = lambda w: pl.BlockSpec((None, tm, w), lambda b, i: (b, i, 0))
    col = lambda w: pl.BlockSpec((None, w, tm), lambda b, i: (b, 0, i))
    full = lambda a: pl.BlockSpec(a.shape, lambda b, i: (0,) * a.ndim)
    sds = jax.ShapeDtypeStruct
    return pl.pallas_call(
        functools.partial(_in_proj_body, logit_scale=logit_scale),
        grid=(B, nt),
        in_specs=[row(D), full(g), full(wqkv), full(wgla), full(wsm), full(wau), full(bal), full(bfr), full(tri)],
        out_specs=[col(FOX_W), col(FOX_W), col(FOX_W), col(FOX_W), row(FOX_W), row(LANES), col(FOX_HEADS),
                   row(GLA_K_W), row(GLA_K_W), row(GLA_V_W), row(GLA_K_W)],
        out_shape=[sds((B, FOX_W, S), F32), sds((B, FOX_W, S), F32), sds((B, FOX_W, S), BF16),
                   sds((B, FOX_W, S), BF16), sds((B, S, FOX_W), BF16), sds((B, S, LANES), BF16),
                   sds((B, FOX_HEADS, S), F32),
                   sds((B, S, GLA_K_W), F32), sds((B, S, GLA_K_W), F32), sds((B, S, GLA_V_W), BF16),
                   sds((B, S, GLA_K_W), F32)],
        scratch_shapes=[pltpu.VMEM((1, LANES), F32)],
        compiler_params=_params(2),
        name="in_proj",
    )(x, g, wqkv, wgla, wsm, wau, bal, bfr, tri)


def _fox_body(k_ref, fa_ref, qT_ref, vT_ref, o_ref, rhs_ref, acc_ref, m_ref, *, t):
    i = pl.program_id(1)
    npair = FOX_HEADS // 2
    ONES = 16

    @pl.when((pl.program_id(0) == 0) & (i == 0))
    def _():
        rr = lax.broadcasted_iota(jnp.int32, (LANES, 2 * t), 0)
        cc = lax.broadcasted_iota(jnp.int32, (LANES, 2 * t), 1)
        head = rr & (FOX_HEADS - 1)
        for p in range(npair):
            mine = head == jnp.where(cc < t, 2 * p, 2 * p + 1)
            neg = jnp.where(rr < F_PARTS * FOX_HEADS, jnp.where(mine, -1.0, 0.0), 0.0)
            rhs_ref[p, LANES:, :] = neg.astype(BF16)
            rhs_ref[p, :FOX_HD, t:] = jnp.zeros((FOX_HD, t), BF16)
            rhs_ref[p, FOX_HD:LANES, :t] = jnp.zeros((FOX_HD, t), BF16)

    for p in range(npair):
        rhs_ref[p, :FOX_HD, :t] = qT_ref[p * LANES:p * LANES + FOX_HD, :]
        rhs_ref[p, FOX_HD:LANES, t:] = qT_ref[p * LANES + FOX_HD:(p + 1) * LANES, :]
    m_ref[...] = jnp.full(m_ref.shape, -jnp.inf, F32)
    acc_ref[...] = jnp.zeros(acc_ref.shape, F32)
    ones = jnp.where(lax.broadcasted_iota(jnp.int32, (ONES, t), 0) == 0, 1.0, 0.0).astype(BF16)

    def run(blocks):
        work = []
        for j, diagonal in blocks:
            c0 = pl.multiple_of(j * t, t)
            fa = fa_ref[pl.ds(c0, t), :]
            work += [(c0, fa, diagonal, p) for p in range(npair)]

        def scores(c0, fa, diagonal, p):
            lhs = jnp.concatenate([k_ref[pl.ds(c0, t), p * LANES:(p + 1) * LANES], fa], axis=1)
            sT = _dot(lhs, rhs_ref[p])
            if diagonal:
                key = lax.broadcasted_iota(jnp.int32, (t, 2 * t), 0)
                qry = lax.broadcasted_iota(jnp.int32, (t, 2 * t), 1)
                sT = jnp.where(key <= jnp.where(qry >= t, qry - t, qry), sT, -jnp.inf)
            return sT

        def soft(p, sT):
            m_old = m_ref[p]
            m_new = jnp.maximum(m_old, jnp.max(sT, axis=0, keepdims=True))
            m_ref[p] = m_new
            return jnp.exp2(m_old - m_new), jnp.exp2(sT - m_new).astype(BF16)

        def pv(c0, p, alpha, pT):
            for e in range(2):
                r0 = p * LANES + e * FOX_HD
                v1 = jnp.concatenate([vT_ref[r0:r0 + FOX_HD, pl.ds(c0, t)], ones], axis=0)
                cols = slice(e * t, (e + 1) * t)
                acc_ref[p, e] = alpha[:, cols] * acc_ref[p, e] + _dot(v1, pT[:, cols])

        n = len(work)
        sT = {x: scores(*work[x]) for x in range(2)}
        done = {}
        for x in range(n):
            done[x] = soft(work[x][3], sT.pop(x))
            if x + 2 < n:
                sT[x + 2] = scores(*work[x + 2])
            if x >= 1:
                pv(work[x - 1][0], work[x - 1][3], *done.pop(x - 1))
        pv(work[n - 1][0], work[n - 1][3], *done.pop(n - 1))

    def two_blocks(jj, carry):
        run([(2 * jj, False), (2 * jj + 1, False)])
        return carry

    lax.fori_loop(0, lax.shift_right_logical(i, 1), two_blocks, 0)

    @pl.when((i & 1) == 1)
    def _():
        run([(i - 1, False), (i, True)])

    @pl.when((i & 1) == 0)
    def _():
        run([(i, True)])

    for p in range(npair):
        halves = []
        for e in range(2):
            acc = acc_ref[p, e]
            halves.append(acc[:FOX_HD] * (1.0 / acc[FOX_HD:FOX_HD + 1]))
        o_ref[:, p * LANES:(p + 1) * LANES] = jnp.concatenate(halves, axis=0).T.astype(o_ref.dtype)


def _fox_prompt(kb, fa, qTb, vTb, t):
    B, S, _ = kb.shape
    npair = FOX_HEADS // 2
    return pl.pallas_call(
        functools.partial(_fox_body, t=t),
        grid=(B, S // t),
        in_specs=[pl.BlockSpec((None, S, FOX_W), lambda b, i: (b, 0, 0)),
                  pl.BlockSpec((None, S, LANES), lambda b, i: (b, 0, 0)),
                  pl.BlockSpec((None, FOX_W, t), lambda b, i: (b, 0, i)),
                  pl.BlockSpec((None, FOX_W, S), lambda b, i: (b, 0, 0))],
        out_specs=pl.BlockSpec((None, t, FOX_W), lambda b, i: (b, i, 0)),
        out_shape=jax.ShapeDtypeStruct((B, S, FOX_W), BF16),
        scratch_shapes=[pltpu.VMEM((npair, 2 * LANES, 2 * t), BF16), pltpu.VMEM((npair, 2, FOX_HD + 16, t), F32),
                        pltpu.VMEM((npair, 1, 2 * t), F32)],
        compiler_params=_params(2),
        name="fox_prompt",
    )(kb, fa, qTb, vTb)


def _gla_body(qg_ref, kg_ref, vg_ref, la_ref, s0_ref, o_ref, sN_ref, st_ref, *, bt, C):
    c = pl.program_id(1)
    KW, VW = GLA_K_W, GLA_V_W
    blk = (lax.broadcasted_iota(jnp.int32, (VW, KW), 0) // GLA_DV
           == lax.broadcasted_iota(jnp.int32, (VW, KW), 1) // GLA_DK)

    @pl.when(c == 0)
    def _():
        for b in range(bt):
            s0 = s0_ref[b]
            rows = [jnp.concatenate([s0[h] if hh == h else jnp.zeros((GLA_DK, GLA_DV), F32)
                                     for hh in range(GLA_HEADS)], axis=1) for h in range(GLA_HEADS)]
            st_ref[b] = jnp.concatenate(rows, axis=0).T

    tri = (lax.broadcasted_iota(jnp.int32, (C, C), 1) <= lax.broadcasted_iota(jnp.int32, (C, C), 0))
    tri_b = tri.astype(BF16)
    tri4 = jnp.concatenate([tri] * GLA_HEADS, axis=0)
    head_of_lane = lax.broadcasted_iota(jnp.int32, (C, KW), 1) // GLA_DK

    def chain(b):
        bc = sum(_dot(tri_b, part) for part in _split3(la_ref[b]))
        yield
        ref = bc[C // 2 - 1:C // 2, :]
        last = bc[C - 1:C, :]
        q = qg_ref[b]
        k = kg_ref[b]
        v = vg_ref[b]
        st = st_ref[b]
        qi = (q * jnp.exp(bc)).astype(BF16)
        qt = q * jnp.exp(bc - ref)
        kt = (k * jnp.exp(ref - bc)).astype(BF16)
        kd = (k * jnp.exp(last - bc)).astype(BF16)
        qm = jnp.concatenate([jnp.where(head_of_lane == h, qt, 0.0) for h in range(GLA_HEADS)],
                             axis=0).astype(BF16)
        a_raw = _dot_nt(qm, kt)
        inter = _dot_nt(qi, st.astype(BF16))
        upd = _dot(v.astype(F32).T.astype(BF16), kd)
        yield
        A = jnp.where(tri4, a_raw, 0.0).astype(BF16)
        intra = jnp.concatenate([_dot(A[h * C:(h + 1) * C], v[:, h * GLA_DV:(h + 1) * GLA_DV])
                                 for h in range(GLA_HEADS)], axis=1)
        yield
        o_ref[b] = inter + intra
        st_ref[b] = jnp.where(blk, st * jnp.exp(last) + upd, 0.0)
        yield

    width = math.gcd(bt, GLA_INTERLEAVE)

    def group(i, _):
        chains = [chain(i * width + r) for r in range(width)]
        for _stage in range(4):
            for ch in chains:
                next(ch)
        return 0

    lax.fori_loop(0, bt // width, group, 0)

    @pl.when(c == pl.num_programs(1) - 1)
    def _():
        for b in range(bt):
            s = st_ref[b].T
            for h in range(GLA_HEADS):
                sN_ref[b, h] = s[h * GLA_DK:(h + 1) * GLA_DK, h * GLA_DV:(h + 1) * GLA_DV]


def _gla(qg, kg, vg, la, s0, bt):
    B, S, _ = qg.shape
    C = GLA_CHUNK
    spec = lambda w: pl.BlockSpec((bt, C, w), lambda g, c: (g, c, 0))
    sspec = pl.BlockSpec((bt, GLA_HEADS, GLA_DK, GLA_DV), lambda g, c: (g, 0, 0, 0))
    return pl.pallas_call(
        functools.partial(_gla_body, bt=bt, C=C),
        grid=(B // bt, S // C),
        in_specs=[spec(GLA_K_W), spec(GLA_K_W), spec(GLA_V_W), spec(GLA_K_W), sspec],
        out_specs=[spec(GLA_V_W), sspec],
        out_shape=[jax.ShapeDtypeStruct((B, S, GLA_V_W), F32),
                   jax.ShapeDtypeStruct((B, GLA_HEADS, GLA_DK, GLA_DV), F32)],
        scratch_shapes=[pltpu.VMEM((bt, GLA_V_W, GLA_K_W), F32)],
        compiler_params=_params(2),
        name="gla_scan",
    )(qg, kg, vg, la, s0)


DECODE_SUBSTEPS = 8
MLP_CHUNKS = 4


def _finish_stages(x_ref, of_ref, og_ref, g1_ref, gn_ref, g2_ref, g3_ref, g4_ref, wog_ref, wgf_ref, wgg_ref, wfo_ref,
                   wgo_ref, wo_ref, wu_ref, wd_ref, y_ref):
    x = x_ref[...]
    h = _rms(x, g1_ref[...]).astype(BF16)
    gate = _dot(h, wog_ref[...])
    gf = _dot(h, wgf_ref[...])
    yield
    gg = _dot(h, wgg_ref[...])
    fo = _dot(of_ref[...], wfo_ref[...])
    o = og_ref[...]
    ys = []
    for hh in range(GLA_HEADS):
        sl = slice(hh * GLA_DV, (hh + 1) * GLA_DV)
        gt = gate[:, sl]
        ys.append(_rms(o[:, sl], gn_ref[...]) * (gt * _sigmoid(gt)))
    y_gla = jnp.concatenate(ys, axis=1).astype(BF16)
    yield
    u = _sigmoid(gf) * fo + _sigmoid(gg) * _dot(y_gla, wgo_ref[...])
    x1 = x + _rms(_dot(u.astype(BF16), wo_ref[...]), g2_ref[...])
    h2 = _rms(x1, g3_ref[...]).astype(BF16)
    yield
    ff = wu_ref.shape[1] // MLP_CHUNKS
    acc = jnp.zeros(x.shape, F32)
    for c in range(MLP_CHUNKS):
        up = jnp.maximum(_dot(h2, wu_ref[:, c * ff:(c + 1) * ff]), 0.0)
        if c == MLP_CHUNKS - 1:
            yield
        acc = acc + _dot((up * up).astype(BF16), wd_ref[c * ff:(c + 1) * ff, :])
        if c < MLP_CHUNKS - 1:
            yield
    y_ref[...] = x1 + _rms(acc, g4_ref[...])
    yield


def _finish_body(*refs):
    for _ in _finish_stages(*refs):
        pass


def _finish_decode_body(pt_ref, *refs, G, NP, RING):
    fin, dec = refs[:16], refs[16:24]
    y_ref, o_ref = refs[24:26]
    scratch = refs[26:]
    i = pl.program_id(0)
    n_sub = pl.num_programs(0) * DECODE_SUBSTEPS
    stages = _finish_stages(*fin, y_ref)
    for r in range(DECODE_SUBSTEPS):
        next(stages)
        _decode_step(pt_ref, dec, o_ref, scratch, i * DECODE_SUBSTEPS + r, n_sub, r, DECODE_SUBSTEPS, G, NP, RING)


def _finish(x, o_fox, o_gla, W, tm):
    N, D = x.shape
    row = lambda w: pl.BlockSpec((tm, w), lambda i: (i, 0))
    full = lambda a: pl.BlockSpec(a.shape, lambda i: (0,) * a.ndim, pipeline_mode=pl.Buffered(1))
    consts = [W[n] for n in ("g1", "gn", "g2", "g3", "g4", "wog", "wgf", "wgg", "wfo", "wgo", "wo", "wu", "wd")]
    return pl.pallas_call(
        _finish_body,
        grid=(N // tm,),
        in_specs=[row(D), row(FOX_W), row(GLA_V_W)] + [full(a) for a in consts],
        out_specs=row(D),
        out_shape=jax.ShapeDtypeStruct((N, D), F32),
        compiler_params=_params(1),
        name="finish",
    )(x, o_fox, o_gla, *consts)


def _decode_step(pt_ref, dec, o_ref, scratch, step, n_steps, r, period, G, NP, RING):
    kc_hbm, vc_hbm, fc_hbm, q_ref, kn_ref, vn_ref, cn_ref, w_ref = dec
    kbuf, vbuf, fbuf, sem, qs_ref, acc_ref, car_ref, m_ref, l_ref = scratch
    ng = NP // G
    g = lax.rem(step, jnp.int32(ng))
    _, _, H, HD, PAGE = kbuf.shape
    reachable = lambda gval: (gval - r) % math.gcd(ng, period) == 0

    def page_copies(s, slot):
        bb, gg = (s // ng, s % ng) if isinstance(s, int) else (lax.div(s, jnp.int32(ng)), lax.rem(s, jnp.int32(ng)))
        out = []
        for j in range(G):
            page = pt_ref[bb, NP - 1 - (gg * G + j)]
            out.append(pltpu.make_async_copy(kc_hbm.at[0, page], kbuf.at[slot, j], sem.at[0, slot]))
            out.append(pltpu.make_async_copy(vc_hbm.at[0, page], vbuf.at[slot, j], sem.at[1, slot]))
            out.append(pltpu.make_async_copy(fc_hbm.at[0, page], fbuf.at[slot, j], sem.at[2, slot]))
        return out

    if r == 0:
        @pl.when(step == 0)
        def _():
            for s in range(RING - 1):
                for cp in page_copies(s, s):
                    cp.start()

    ahead = step + (RING - 1)

    @pl.when(ahead < n_steps)
    def _():
        for cp in page_copies(ahead, lax.rem(ahead, jnp.int32(RING))):
            cp.start()

    slot = lax.rem(step, jnp.int32(RING))
    for cp in page_copies(step, slot):
        cp.wait()
    k_refs = [kbuf.at[slot, j] for j in range(G)]
    v_refs = [vbuf.at[slot, j] for j in range(G)]
    f_refs = [fbuf.at[slot, j] for j in range(G)]

    if reachable(0):
        @pl.when(g == 0)
        def _():
            qs_ref[...] = jnp.broadcast_to(q_ref[...], (H, HD, PAGE))
            car_ref[...] = jnp.broadcast_to(cn_ref[...], (H, PAGE))
            m_ref[...] = jnp.full(m_ref.shape, -jnp.inf, F32)
            l_ref[...] = jnp.zeros(l_ref.shape, F32)
            acc_ref[...] = jnp.zeros(acc_ref.shape, F32)

    lf = jnp.concatenate([f_refs[j][...] for j in range(G)], axis=0)
    w = w_ref[...]
    suf = sum(_dot(part, w) for part in _split3(lf))
    carry = car_ref[...]
    bias = []
    for j in range(G):
        bias.append(suf[j * H:(j + 1) * H, :PAGE] + carry)
        carry = carry + suf[j * H:(j + 1) * H, PAGE:]
    car_ref[...] = carry

    for h in range(H):
        qh = qs_ref[h]
        rows = [jnp.sum(k_refs[j][h] * qh, axis=0, keepdims=True) + bias[j][h:h + 1, :] for j in range(G)]
        m_old = m_ref[h]
        m_new = jnp.maximum(m_old, functools.reduce(jnp.maximum, rows))
        alpha = jnp.exp(m_old - m_new)
        a = acc_ref[h] * alpha
        ls = l_ref[h] * alpha
        for j in range(G):
            pr = jnp.exp(rows[j] - m_new)
            ls = ls + pr
            a = a + v_refs[j][h] * pr
        acc_ref[h] = a
        l_ref[h] = ls
        m_ref[h] = m_new

    if reachable(ng - 1):
        @pl.when(g == ng - 1)
        def _():
            for h in range(H):
                ln = jnp.sum(qs_ref[h] * kn_ref[h], axis=0, keepdims=True)
                m = m_ref[h]
                mf = jnp.maximum(jnp.max(m, axis=1, keepdims=True), ln)
                wl = jnp.exp(m - mf)
                pn = jnp.exp(ln - mf)[:, 0:1]
                den = jnp.sum(l_ref[h] * wl, axis=1, keepdims=True) + pn
                num = jnp.sum(acc_ref[h] * wl, axis=1, keepdims=True) + pn * vn_ref[h]
                o_ref[h] = num / den


def _finish_decode(x, o_fox, o_gla, W, tm, kc, vc, fc, page_table, q, kn, vn, cn, G):
    N, D = x.shape
    DB, NP = page_table.shape
    _, _, H, HD, PAGE = kc.shape
    ng = NP // G
    n_tiles = N // tm
    assert ng % DECODE_SUBSTEPS == 0 and n_tiles * DECODE_SUBSTEPS == DB * ng
    tiles_per_seq = ng // DECODE_SUBSTEPS
    pos = lax.broadcasted_iota(jnp.int32, (PAGE, 2 * PAGE), 0)
    lane = lax.broadcasted_iota(jnp.int32, (PAGE, 2 * PAGE), 1)
    wsuf = jnp.where(lane < PAGE, pos > lane, True).astype(BF16)

    row = lambda w: pl.BlockSpec((tm, w), lambda i, pt: (i, 0))
    full = lambda a: pl.BlockSpec(a.shape, lambda i, pt: (0,) * a.ndim, pipeline_mode=pl.Buffered(1))
    consts = [W[n] for n in ("g1", "gn", "g2", "g3", "g4", "wog", "wgf", "wgg", "wfo", "wgo", "wo", "wu", "wd")]
    hbm = pl.BlockSpec(memory_space=pl.ANY)
    col = pl.BlockSpec((None, H, HD, 1), lambda i, pt: (i // tiles_per_seq, 0, 0, 0))
    specs = ([row(D), row(FOX_W), row(GLA_V_W)] + [full(a) for a in consts]
             + [hbm, hbm, hbm, col, col, col, pl.BlockSpec((None, H, 1), lambda i, pt: (i // tiles_per_seq, 0, 0)),
                full(wsuf)])
    return pl.pallas_call(
        functools.partial(_finish_decode_body, G=G, NP=NP, RING=DECODE_RING),
        grid_spec=pltpu.PrefetchScalarGridSpec(
            num_scalar_prefetch=1,
            grid=(n_tiles,),
            in_specs=specs,
            out_specs=[row(D), col],
            scratch_shapes=[pltpu.VMEM((DECODE_RING, G, H, HD, PAGE), F32),
                            pltpu.VMEM((DECODE_RING, G, H, HD, PAGE), F32),
                            pltpu.VMEM((DECODE_RING, G, H, PAGE), F32),
                            pltpu.SemaphoreType.DMA((3, DECODE_RING)),
                            pltpu.VMEM((H, HD, PAGE), F32), pltpu.VMEM((H, HD, PAGE), F32),
                            pltpu.VMEM((H, PAGE), F32), pltpu.VMEM((H, 1, PAGE), F32),
                            pltpu.VMEM((H, 1, PAGE), F32)]),
        out_shape=[jax.ShapeDtypeStruct((N, D), F32), jax.ShapeDtypeStruct((DB, H, HD, 1), F32)],
        compiler_params=_params(1),
        name="finish_decode",
    )(page_table, x, o_fox, o_gla, *consts, kc, vc, fc, q, kn, vn, cn, wsuf)


def _layer_weights(l, g_pre_mix, w_in, b_f, w_alpha_up, b_alpha, g_gla_norm, w_fox_out, w_gla_out, w_o,
                   g_post_mix, g_pre_mlp, w_up, w_down, g_post_mlp):
    sizes = (FOX_W, FOX_W, FOX_W, FOX_HEADS, GLA_K_W, GLA_K_W, GLA_V_W, GLA_V_W, GLA_RANK)
    off = [0]
    for s in sizes:
        off.append(off[-1] + s)
    w = w_in[l]
    D = w.shape[0]
    wb = w.astype(BF16)
    wqkv = wb[:, :off[3]]
    wgla = wb[:, off[4]:off[7]]
    wog = wb[:, off[7]:off[8]]
    wgf = wb[:, off[9]:off[9] + D]
    wgg = wb[:, off[9] + D:off[9] + 2 * D]
    nf = F_PARTS * FOX_HEADS
    pad = LANES - nf - GLA_RANK
    wsm = jnp.concatenate([wb[:, off[3]:off[4]]] * F_PARTS + [wb[:, off[8]:off[9]], jnp.zeros((D, pad), BF16)], axis=1)
    wau = jnp.concatenate([jnp.zeros((nf, GLA_K_W), BF16), w_alpha_up[l].astype(BF16),
                           jnp.zeros((pad, GLA_K_W), BF16)], axis=0)
    bfr = jnp.concatenate([b_f[l]] * F_PARTS + [jnp.zeros((LANES - nf,), F32)])[None, :]
    r = lambda a: a[l][None, :]
    return dict(g1=r(g_pre_mix), wqkv=wqkv, wgla=wgla, wsm=wsm, wau=wau, bal=r(b_alpha), bfr=bfr,
                gn=r(g_gla_norm), wog=wog, wgf=wgf, wgg=wgg, wfo=w_fox_out[l].astype(BF16),
                wgo=w_gla_out[l].astype(BF16), wo=w_o[l].astype(BF16), g2=r(g_post_mix), g3=r(g_pre_mlp),
                wu=w_up[l].astype(BF16), wd=w_down[l].astype(BF16), g4=r(g_post_mlp))


def kernel(x_prompt, x_sample, cache_k, cache_v, cache_logf, state_gla, page_table, g_pre_mix, w_in, b_f, w_alpha_up, b_alpha, g_gla_norm, w_fox_out, w_gla_out, w_o, g_post_mix, g_pre_mlp, w_up, w_down, g_post_mlp):
    B, S, D = x_prompt.shape
    DB = x_sample.shape[0]
    depth = w_in.shape[0]
    assert depth == 1 and x_sample.shape[1] == 1
    W = _layer_weights(0, g_pre_mix, w_in, b_f, w_alpha_up, b_alpha, g_gla_norm, w_fox_out, w_gla_out, w_o,
                       g_post_mix, g_pre_mlp, w_up, w_down, g_post_mlp)
    proj = lambda x, tm, sc: _in_proj(x, W["g1"], W["wqkv"], W["wgla"], W["wsm"], W["wau"], W["bal"], W["bfr"], tm, sc)

    PADT = LANES
    xs = jnp.zeros((1, PADT, D), F32).at[0, :DB].set(x_sample[:, 0])
    kT, vT, _, qTb, _, _, lfT, qg_s, kg_s, vg_s, la_s = proj(xs, PADT, 1.0)
    k_s = kT[0].T[:DB].reshape(DB, FOX_HEADS, FOX_HD)
    v_s = vT[0].T[:DB].reshape(DB, FOX_HEADS, FOX_HD)
    lf_s = lfT[0].T[:DB]
    q_s = qTb[0].T[:DB].astype(F32).reshape(DB, FOX_HEADS, FOX_HD)
    kc = jnp.transpose(cache_k, (0, 1, 3, 4, 2))
    vc = jnp.transpose(cache_v, (0, 1, 3, 4, 2))
    fc = jnp.transpose(cache_logf, (0, 1, 3, 2))

    kT, vT, vTb, qTb, kb, fa, lfT, qg, kg, vg, la = proj(x_prompt, ROW_TILE, LOG2E)
    o_fox = _fox_prompt(kb, fa, qTb, vTb, ATT_TILE)
    o_gla, s_p = _gla(qg, kg, vg, la, jnp.zeros((B, GLA_HEADS, GLA_DK, GLA_DV), F32), B)
    y_p, o_fs = _finish_decode(x_prompt.reshape(B * S, D), o_fox.reshape(B * S, FOX_W), o_gla.reshape(B * S, GLA_V_W),
                               W, ROW_TILE, kc, vc, fc, page_table, q_s[..., None], k_s[..., None], v_s[..., None],
                               lf_s[..., None], PAGES_PER_STEP)
    y_p = y_p.reshape(B, S, D)
    to_tok = lambda a: jnp.transpose(a.reshape(1, B, FOX_HEADS, FOX_HD, S), (0, 1, 4, 2, 3))
    k_p, v_p = to_tok(kT), to_tok(vT)
    lf_p = jnp.transpose(lfT, (0, 2, 1))[None]

    o_fox_s = jnp.zeros((PADT, FOX_W), BF16).at[:DB].set(o_fs.reshape(DB, FOX_W).astype(BF16))
    C = GLA_CHUNK
    pad_c = lambda a: jnp.zeros((DB, C, a.shape[-1]), a.dtype).at[:, 0].set(a[0, :DB])
    o_gs, s_s = _gla(pad_c(qg_s), pad_c(kg_s), pad_c(vg_s), pad_c(la_s), state_gla[0], DB // 2)
    o_gla_s = jnp.zeros((PADT, GLA_V_W), F32).at[:DB].set(o_gs[:, 0])
    y_s = _finish(xs[0], o_fox_s, o_gla_s, W, PADT)[:DB].reshape(DB, 1, D)

    return (y_p, y_s, k_p, v_p, lf_p, s_p[None],
            k_s[None, :, None], v_s[None, :, None], lf_s[None, :, None], s_s[None])
```

```python
import functools
import math

import jax
import jax.numpy as jnp
from jax import lax
from jax.experimental import pallas as pl
from jax.experimental.pallas import tpu as pltpu

F32 = jnp.float32
BF16 = jnp.bfloat16

FOX_HEADS = 8
FOX_HD = 64
FOX_W = FOX_HEADS * FOX_HD
GLA_HEADS = 4
GLA_DK = 64
GLA_DV = 128
GLA_K_W = GLA_HEADS * GLA_DK
GLA_V_W = GLA_HEADS * GLA_DV
GLA_RANK = 16
GLA_GATE_NORM = 16.0
GLA_CHUNK = 64
EPS = 1e-6
LANES = 128
VMEM_LIMIT = 56 * 1024 * 1024
LOG2E = 1.4426950408889634
F_PARTS = 3

ROW_TILE = 512
ATT_TILE = 256
PAGES_PER_STEP = 8
DECODE_RING = 2
CUM_BLOCK = 128
GLA_INTERLEAVE = 8


def _params(n_axes, flags=None):
    return pltpu.CompilerParams(dimension_semantics=("arbitrary",) * n_axes,
                                vmem_limit_bytes=VMEM_LIMIT, flags=flags)


def _log_sigmoid(z):
    return jnp.minimum(z, 0.0) - jnp.log1p(jnp.exp(-jnp.abs(z)))


def _sigmoid(z):
    return 1.0 / (1.0 + jnp.exp(-z))


def _rms(x, g):
    return x * lax.rsqrt(jnp.mean(x * x, axis=-1, keepdims=True) + EPS) * g


def _dot(a, b):
    return jnp.dot(a, b, preferred_element_type=F32)


def _dot_nt(a, b):
    return lax.dot_general(a, b, (((1,), (1,)), ((), ())), preferred_element_type=F32)


def _split3(x):
    hi = x.astype(BF16)
    r1 = x - hi.astype(F32)
    mid = r1.astype(BF16)
    lo = (r1 - mid.astype(F32)).astype(BF16)
    return hi, mid, lo


def _in_proj_body(x_ref, g_ref, wqkv_ref, wgla_ref, wsm_ref, wau_ref, bal_ref, bf_ref, tri_ref,
                  kT_ref, vT_ref, vTb_ref, qTb_ref, kb_ref, fa_ref, lfT_ref, qg_ref, kg_ref, vg_ref, la_ref,
                  car_ref, *, logit_scale):
    @pl.when(pl.program_id(1) == 0)
    def _():
        car_ref[...] = jnp.zeros_like(car_ref)

    h = _rms(x_ref[...], g_ref[...]).astype(BF16)
    qkv = _dot(h, wqkv_ref[...])
    q = qkv[:, :FOX_W] * (FOX_HD ** -0.5 * logit_scale)
    k = qkv[:, FOX_W:2 * FOX_W]
    v = qkv[:, 2 * FOX_W:]
    qTb_ref[...] = q.T.astype(BF16)
    kb_ref[...] = k.astype(BF16)
    kT_ref[...] = k.T
    vT = v.T
    vT_ref[...] = vT
    vTb_ref[...] = vT.astype(BF16)
    gl = _dot(h, wgla_ref[...])
    qg_ref[...] = gl[:, :GLA_K_W] * (GLA_DK ** -0.5)
    kg_ref[...] = gl[:, GLA_K_W:2 * GLA_K_W]
    vg_ref[...] = gl[:, 2 * GLA_K_W:].astype(BF16)
    sm = _dot(h, wsm_ref[...])
    lf = _log_sigmoid(sm + bf_ref[...])
    lfT_ref[...] = lf.T[:FOX_HEADS, :]
    z = _dot(sm.astype(BF16), wau_ref[...]) + bal_ref[...]
    la_ref[...] = _log_sigmoid(z) * (1.0 / GLA_GATE_NORM)
    tri = tri_ref[...]
    parts = _split3(lf)
    run = car_ref[...]
    blocks = []
    for r0 in range(0, lf.shape[0], CUM_BLOCK):
        blocks.append(run + sum(_dot(tri, part[r0:r0 + CUM_BLOCK]) for part in parts))
        run = blocks[-1][CUM_BLOCK - 1:, :]
    F = jnp.concatenate(blocks, axis=0)
    car_ref[...] = run
    hi, mid, lo = _split3(F * logit_scale)
    lane = lax.broadcasted_iota(jnp.int32, F.shape, 1)
    zero = jnp.zeros_like(hi)
    fa_ref[...] = jnp.where(lane < FOX_HEADS, hi,
                            jnp.where(lane < 2 * FOX_HEADS, mid, jnp.where(lane < 3 * FOX_HEADS, lo, zero)))


def _in_proj(x, g, wqkv, wgla, wsm, wau, bal, bfr, tm, logit_scale):
    B, S, D = x.shape
    nt = S // tm
    assert tm % CUM_BLOCK == 0
    tri = (lax.broadcasted_iota(jnp.int32, (CUM_BLOCK, CUM_BLOCK), 1)
           <= lax.broadcasted_iota(jnp.int32, (CUM_BLOCK, CUM_BLOCK), 0)).astype(BF16)
    row = lambda w: pl.BlockSpec((None, tm, w), lambda b, i: (b, i, 0))
    col = lambda w: pl.BlockSpec((None, w, tm), lambda b, i: (b, 0, i))
    full = lambda a: pl.BlockSpec(a.shape, lambda b, i: (0,) * a.ndim)
    sds = jax.ShapeDtypeStruct
    return pl.pallas_call(
        functools.partial(_in_proj_body, logit_scale=logit_scale),
        grid=(B, nt),
        in_specs=[row(D), full(g), full(wqkv), full(wgla), full(wsm), full(wau), full(bal), full(bfr), full(tri)],
        out_specs=[col(FOX_W), col(FOX_W), col(FOX_W), col(FOX_W), row(FOX_W), row(LANES), col(FOX_HEADS),
                   row(GLA_K_W), row(GLA_K_W), row(GLA_V_W), row(GLA_K_W)],
        out_shape=[sds((B, FOX_W, S), F32), sds((B, FOX_W, S), F32), sds((B, FOX_W, S), BF16),
                   sds((B, FOX_W, S), BF16), sds((B, S, FOX_W), BF16), sds((B, S, LANES), BF16),
                   sds((B, FOX_HEADS, S), F32),
                   sds((B, S, GLA_K_W), F32), sds((B, S, GLA_K_W), F32), sds((B, S, GLA_V_W), BF16),
                   sds((B, S, GLA_K_W), F32)],
        scratch_shapes=[pltpu.VMEM((1, LANES), F32)],
        compiler_params=_params(2),
        name="in_proj",
    )(x, g, wqkv, wgla, wsm, wau, bal, bfr, tri)


def _fox_body(k_ref, fa_ref, qT_ref, vT_ref, o_ref, rhs_ref, acc_ref, m_ref, *, t):
    i = pl.program_id(1)
    npair = FOX_HEADS // 2
    ONES = 16

    @pl.when((pl.program_id(0) == 0) & (i == 0))
    def _():
        rr = lax.broadcasted_iota(jnp.int32, (LANES, 2 * t), 0)
        cc = lax.broadcasted_iota(jnp.int32, (LANES, 2 * t), 1)
        head = rr & (FOX_HEADS - 1)
        for p in range(npair):
            mine = head == jnp.where(cc < t, 2 * p, 2 * p + 1)
            neg = jnp.where(rr < F_PARTS * FOX_HEADS, jnp.where(mine, -1.0, 0.0), 0.0)
            rhs_ref[p, LANES:, :] = neg.astype(BF16)
            rhs_ref[p, :FOX_HD, t:] = jnp.zeros((FOX_HD, t), BF16)
            rhs_ref[p, FOX_HD:LANES, :t] = jnp.zeros((FOX_HD, t), BF16)

    for p in range(npair):
        rhs_ref[p, :FOX_HD, :t] = qT_ref[p * LANES:p * LANES + FOX_HD, :]
        rhs_ref[p, FOX_HD:LANES, t:] = qT_ref[p * LANES + FOX_HD:(p + 1) * LANES, :]
    m_ref[...] = jnp.full(m_ref.shape, -jnp.inf, F32)
    acc_ref[...] = jnp.zeros(acc_ref.shape, F32)
    ones = jnp.where(lax.broadcasted_iota(jnp.int32, (ONES, t), 0) == 0, 1.0, 0.0).astype(BF16)

    def run(blocks):
        work = []
        for j, diagonal in blocks:
            c0 = pl.multiple_of(j * t, t)
            fa = fa_ref[pl.ds(c0, t), :]
            work += [(c0, fa, diagonal, p) for p in range(npair)]

        def scores(c0, fa, diagonal, p):
            lhs = jnp.concatenate([k_ref[pl.ds(c0, t), p * LANES:(p + 1) * LANES], fa], axis=1)
            sT = _dot(lhs, rhs_ref[p])
            if diagonal:
                key = lax.broadcasted_iota(jnp.int32, (t, 2 * t), 0)
                qry = lax.broadcasted_iota(jnp.int32, (t, 2 * t), 1)
                sT = jnp.where(key <= jnp.where(qry >= t, qry - t, qry), sT, -jnp.inf)
            return sT

        def soft(p, sT):
            m_old = m_ref[p]
            m_new = jnp.maximum(m_old, jnp.max(sT, axis=0, keepdims=True))
            m_ref[p] = m_new
            return jnp.exp2(m_old - m_new), jnp.exp2(sT - m_new).astype(BF16)

        def pv(c0, p, alpha, pT):
            for e in range(2):
                r0 = p * LANES + e * FOX_HD
                v1 = jnp.concatenate([vT_ref[r0:r0 + FOX_HD, pl.ds(c0, t)], ones], axis=0)
                cols = slice(e * t, (e + 1) * t)
                acc_ref[p, e] = alpha[:, cols] * acc_ref[p, e] + _dot(v1, pT[:, cols])

        n = len(work)
        sT = {x: scores(*work[x]) for x in range(2)}
        done = {}
        for x in range(n):
            done[x] = soft(work[x][3], sT.pop(x))
            if x + 2 < n:
                sT[x + 2] = scores(*work[x + 2])
            if x >= 1:
                pv(work[x - 1][0], work[x - 1][3], *done.pop(x - 1))
        pv(work[n - 1][0], work[n - 1][3], *done.pop(n - 1))

    def two_blocks(jj, carry):
        run([(2 * jj, False), (2 * jj + 1, False)])
        return carry

    lax.fori_loop(0, lax.shift_right_logical(i, 1), two_blocks, 0)

    @pl.when((i & 1) == 1)
    def _():
        run([(i - 1, False), (i, True)])

    @pl.when((i & 1) == 0)
    def _():
        run([(i, True)])

    for p in range(npair):
        halves = []
        for e in range(2):
            acc = acc_ref[p, e]
            halves.append(acc[:FOX_HD] * (1.0 / acc[FOX_HD:FOX_HD + 1]))
        o_ref[:, p * LANES:(p + 1) * LANES] = jnp.concatenate(halves, axis=0).T.astype(o_ref.dtype)


def _fox_prompt(kb, fa, qTb, vTb, t):
    B, S, _ = kb.shape
    npair = FOX_HEADS // 2
    return pl.pallas_call(
        functools.partial(_fox_body, t=t),
        grid=(B, S // t),
        in_specs=[pl.BlockSpec((None, S, FOX_W), lambda b, i: (b, 0, 0)),
                  pl.BlockSpec((None, S, LANES), lambda b, i: (b, 0, 0)),
                  pl.BlockSpec((None, FOX_W, t), lambda b, i: (b, 0, i)),
                  pl.BlockSpec((None, FOX_W, S), lambda b, i: (b, 0, 0))],
        out_specs=pl.BlockSpec((None, t, FOX_W), lambda b, i: (b, i, 0)),
        out_shape=jax.ShapeDtypeStruct((B, S, FOX_W), BF16),
        scratch_shapes=[pltpu.VMEM((npair, 2 * LANES, 2 * t), BF16), pltpu.VMEM((npair, 2, FOX_HD + 16, t), F32),
                        pltpu.VMEM((npair, 1, 2 * t), F32)],
        compiler_params=_params(2),
        name="fox_prompt",
    )(kb, fa, qTb, vTb)


def _gla_body(qg_ref, kg_ref, vg_ref, la_ref, s0_ref, o_ref, sN_ref, st_ref, *, bt, C):
    c = pl.program_id(1)
    KW, VW = GLA_K_W, GLA_V_W
    blk = (lax.broadcasted_iota(jnp.int32, (VW, KW), 0) // GLA_DV
           == lax.broadcasted_iota(jnp.int32, (VW, KW), 1) // GLA_DK)

    @pl.when(c == 0)
    def _():
        for b in range(bt):
            s0 = s0_ref[b]
            rows = [jnp.concatenate([s0[h] if hh == h else jnp.zeros((GLA_DK, GLA_DV), F32)
                                     for hh in range(GLA_HEADS)], axis=1) for h in range(GLA_HEADS)]
            st_ref[b] = jnp.concatenate(rows, axis=0).T

    tri = (lax.broadcasted_iota(jnp.int32, (C, C), 1) <= lax.broadcasted_iota(jnp.int32, (C, C), 0))
    tri_b = tri.astype(BF16)
    tri4 = jnp.concatenate([tri] * GLA_HEADS, axis=0)
    head_of_lane = lax.broadcasted_iota(jnp.int32, (C, KW), 1) // GLA_DK

    def chain(b):
        bc = sum(_dot(tri_b, part) for part in _split3(la_ref[b]))
        yield
        ref = bc[C // 2 - 1:C // 2, :]
        last = bc[C - 1:C, :]
        q = qg_ref[b]
        k = kg_ref[b]
        v = vg_ref[b]
        st = st_ref[b]
        qi = (q * jnp.exp(bc)).astype(BF16)
        qt = q * jnp.exp(bc - ref)
        kt = (k * jnp.exp(ref - bc)).astype(BF16)
        kd = (k * jnp.exp(last - bc)).astype(BF16)
        qm = jnp.concatenate([jnp.where(head_of_lane == h, qt, 0.0) for h in range(GLA_HEADS)],
                             axis=0).astype(BF16)
        a_raw = _dot_nt(qm, kt)
        inter = _dot_nt(qi, st.astype(BF16))
        upd = _dot(v.astype(F32).T.astype(BF16), kd)
        yield
        A = jnp.where(tri4, a_raw, 0.0).astype(BF16)
        intra = jnp.concatenate([_dot(A[h * C:(h + 1) * C], v[:, h * GLA_DV:(h + 1) * GLA_DV])
                                 for h in range(GLA_HEADS)], axis=1)
        yield
        o_ref[b] = inter + intra
        st_ref[b] = jnp.where(blk, st * jnp.exp(last) + upd, 0.0)
        yield

    width = math.gcd(bt, GLA_INTERLEAVE)

    def group(i, _):
        chains = [chain(i * width + r) for r in range(width)]
        for _stage in range(4):
            for ch in chains:
                next(ch)
        return 0

    lax.fori_loop(0, bt // width, group, 0)

    @pl.when(c == pl.num_programs(1) - 1)
    def _():
        for b in range(bt):
            s = st_ref[b].T
            for h in range(GLA_HEADS):
                sN_ref[b, h] = s[h * GLA_DK:(h + 1) * GLA_DK, h * GLA_DV:(h + 1) * GLA_DV]


def _gla(qg, kg, vg, la, s0, bt):
    B, S, _ = qg.shape
    C = GLA_CHUNK
    spec = lambda w: pl.BlockSpec((bt, C, w), lambda g, c: (g, c, 0))
    sspec = pl.BlockSpec((bt, GLA_HEADS, GLA_DK, GLA_DV), lambda g, c: (g, 0, 0, 0))
    return pl.pallas_call(
        functools.partial(_gla_body, bt=bt, C=C),
        grid=(B // bt, S // C),
        in_specs=[spec(GLA_K_W), spec(GLA_K_W), spec(GLA_V_W), spec(GLA_K_W), sspec],
        out_specs=[spec(GLA_V_W), sspec],
        out_shape=[jax.ShapeDtypeStruct((B, S, GLA_V_W), F32),
                   jax.ShapeDtypeStruct((B, GLA_HEADS, GLA_DK, GLA_DV), F32)],
        scratch_shapes=[pltpu.VMEM((bt, GLA_V_W, GLA_K_W), F32)],
        compiler_params=_params(2),
        name="gla_scan",
    )(qg, kg, vg, la, s0)


DECODE_SUBSTEPS = 8
MLP_CHUNKS = 4


def _finish_stages(x_ref, of_ref, og_ref, g1_ref, gn_ref, g2_ref, g3_ref, g4_ref, wog_ref, wgf_ref, wgg_ref, wfo_ref,
                   wgo_ref, wo_ref, wu_ref, wd_ref, y_ref):
    x = x_ref[...]
    h = _rms(x, g1_ref[...]).astype(BF16)
    gate = _dot(h, wog_ref[...])
    gf = _dot(h, wgf_ref[...])
    yield
    gg = _dot(h, wgg_ref[...])
    fo = _dot(of_ref[...], wfo_ref[...])
    o = og_ref[...]
    ys = []
    for hh in range(GLA_HEADS):
        sl = slice(hh * GLA_DV, (hh + 1) * GLA_DV)
        gt = gate[:, sl]
        ys.append(_rms(o[:, sl], gn_ref[...]) * (gt * _sigmoid(gt)))
    y_gla = jnp.concatenate(ys, axis=1).astype(BF16)
    yield
    u = _sigmoid(gf) * fo + _sigmoid(gg) * _dot(y_gla, wgo_ref[...])
    x1 = x + _rms(_dot(u.astype(BF16), wo_ref[...]), g2_ref[...])
    h2 = _rms(x1, g3_ref[...]).astype(BF16)
    yield
    ff = wu_ref.shape[1] // MLP_CHUNKS
    acc = jnp.zeros(x.shape, F32)
    for c in range(MLP_CHUNKS):
        up = jnp.maximum(_dot(h2, wu_ref[:, c * ff:(c + 1) * ff]), 0.0)
        if c == MLP_CHUNKS - 1:
            yield
        acc = acc + _dot((up * up).astype(BF16), wd_ref[c * ff:(c + 1) * ff, :])
        if c < MLP_CHUNKS - 1:
            yield
    y_ref[...] = x1 + _rms(acc, g4_ref[...])
    yield


def _finish_body(*refs):
    for _ in _finish_stages(*refs):
        pass


def _finish_decode_body(pt_ref, *refs, G, NP, RING):
    fin, dec = refs[:16], refs[16:24]
    y_ref, o_ref = refs[24:26]
    scratch = refs[26:]
    i = pl.program_id(0)
    n_sub = pl.num_programs(0) * DECODE_SUBSTEPS
    stages = _finish_stages(*fin, y_ref)
    for r in range(DECODE_SUBSTEPS):
        next(stages)
        _decode_step(pt_ref, dec, o_ref, scratch, i * DECODE_SUBSTEPS + r, n_sub, r, DECODE_SUBSTEPS, G, NP, RING)


def _finish(x, o_fox, o_gla, W, tm):
    N, D = x.shape
    row = lambda w: pl.BlockSpec((tm, w), lambda i: (i, 0))
    full = lambda a: pl.BlockSpec(a.shape, lambda i: (0,) * a.ndim, pipeline_mode=pl.Buffered(1))
    consts = [W[n] for n in ("g1", "gn", "g2", "g3", "g4", "wog", "wgf", "wgg", "wfo", "wgo", "wo", "wu", "wd")]
    return pl.pallas_call(
        _finish_body,
        grid=(N // tm,),
        in_specs=[row(D), row(FOX_W), row(GLA_V_W)] + [full(a) for a in consts],
        out_specs=row(D),
        out_shape=jax.ShapeDtypeStruct((N, D), F32),
        compiler_params=_params(1),
        name="finish",
    )(x, o_fox, o_gla, *consts)


def _decode_step(pt_ref, dec, o_ref, scratch, step, n_steps, r, period, G, NP, RING):
    kc_hbm, vc_hbm, fc_hbm, q_ref, kn_ref, vn_ref, cn_ref, w_ref = dec
    kbuf, vbuf, fbuf, sem, qs_ref, acc_ref, car_ref, m_ref, l_ref = scratch
    ng = NP // G
    g = lax.rem(step, jnp.int32(ng))
    _, _, H, HD, PAGE = kbuf.shape
    reachable = lambda gval: (gval - r) % math.gcd(ng, period) == 0

    def page_copies(s, slot):
        bb, gg = (s // ng, s % ng) if isinstance(s, int) else (lax.div(s, jnp.int32(ng)), lax.rem(s, jnp.int32(ng)))
        out = []
        for j in range(G):
            page = pt_ref[bb, NP - 1 - (gg * G + j)]
            out.append(pltpu.make_async_copy(kc_hbm.at[0, page], kbuf.at[slot, j], sem.at[0, slot]))
            out.append(pltpu.make_async_copy(vc_hbm.at[0, page], vbuf.at[slot, j], sem.at[1, slot]))
            out.append(pltpu.make_async_copy(fc_hbm.at[0, page], fbuf.at[slot, j], sem.at[2, slot]))
        return out

    if r == 0:
        @pl.when(step == 0)
        def _():
            for s in range(RING - 1):
                for cp in page_copies(s, s):
                    cp.start()

    ahead = step + (RING - 1)
    for cp in page_copies(jnp.minimum(ahead, n_steps - 1), lax.rem(ahead, jnp.int32(RING))):
        cp.start()

    slot = lax.rem(step, jnp.int32(RING))
    for cp in page_copies(step, slot):
        cp.wait()
    k_refs = [kbuf.at[slot, j] for j in range(G)]
    v_refs = [vbuf.at[slot, j] for j in range(G)]
    f_refs = [fbuf.at[slot, j] for j in range(G)]

    if reachable(0):
        @pl.when(g == 0)
        def _():
            qs_ref[...] = jnp.broadcast_to(q_ref[...], (H, HD, PAGE))
            car_ref[...] = jnp.broadcast_to(cn_ref[...], (H, PAGE))
            m_ref[...] = jnp.full(m_ref.shape, -jnp.inf, F32)
            l_ref[...] = jnp.zeros(l_ref.shape, F32)
            acc_ref[...] = jnp.zeros(acc_ref.shape, F32)

    lf = jnp.concatenate([f_refs[j][...] for j in range(G)], axis=0)
    w = w_ref[...]
    suf = sum(_dot(part, w) for part in _split3(lf))
    carry = car_ref[...]
    bias = []
    for j in range(G):
        bias.append(suf[j * H:(j + 1) * H, :PAGE] + carry)
        carry = carry + suf[j * H:(j + 1) * H, PAGE:]
    car_ref[...] = carry

    for h in range(H):
        qh = qs_ref[h]
        rows = [jnp.sum(k_refs[j][h] * qh, axis=0, keepdims=True) + bias[j][h:h + 1, :] for j in range(G)]
        m_old = m_ref[h]
        m_new = jnp.maximum(m_old, functools.reduce(jnp.maximum, rows))
        alpha = jnp.exp(m_old - m_new)
        a = acc_ref[h] * alpha
        ls = l_ref[h] * alpha
        for j in range(G):
            pr = jnp.exp(rows[j] - m_new)
            ls = ls + pr
            a = a + v_refs[j][h] * pr
        acc_ref[h] = a
        l_ref[h] = ls
        m_ref[h] = m_new

    if reachable(ng - 1):
        @pl.when(g == ng - 1)
        def _():
            for h in range(H):
                ln = jnp.sum(qs_ref[h] * kn_ref[h], axis=0, keepdims=True)
                m = m_ref[h]
                mf = jnp.maximum(jnp.max(m, axis=1, keepdims=True), ln)
                wl = jnp.exp(m - mf)
                pn = jnp.exp(ln - mf)[:, 0:1]
                den = jnp.sum(l_ref[h] * wl, axis=1, keepdims=True) + pn
                num = jnp.sum(acc_ref[h] * wl, axis=1, keepdims=True) + pn * vn_ref[h]
                o_ref[h] = num / den

    if r >= period - (RING - 1):
        @pl.when(step >= n_steps - (RING - 1))
        def _():
            for cp in page_copies(n_steps - 1, lax.rem(ahead, jnp.int32(RING))):
                cp.wait()


def _finish_decode(x, o_fox, o_gla, W, tm, kc, vc, fc, page_table, q, kn, vn, cn, G):
    N, D = x.shape
    DB, NP = page_table.shape
    _, _, H, HD, PAGE = kc.shape
    ng = NP // G
    n_tiles = N // tm
    assert ng % DECODE_SUBSTEPS == 0 and n_tiles * DECODE_SUBSTEPS == DB * ng
    tiles_per_seq = ng // DECODE_SUBSTEPS
    pos = lax.broadcasted_iota(jnp.int32, (PAGE, 2 * PAGE), 0)
    lane = lax.broadcasted_iota(jnp.int32, (PAGE, 2 * PAGE), 1)
    wsuf = jnp.where(lane < PAGE, pos > lane, True).astype(BF16)

    row = lambda w: pl.BlockSpec((tm, w), lambda i, pt: (i, 0))
    full = lambda a: pl.BlockSpec(a.shape, lambda i, pt: (0,) * a.ndim, pipeline_mode=pl.Buffered(1))
    consts = [W[n] for n in ("g1", "gn", "g2", "g3", "g4", "wog", "wgf", "wgg", "wfo", "wgo", "wo", "wu", "wd")]
    hbm = pl.BlockSpec(memory_space=pl.ANY)
    col = pl.BlockSpec((None, H, HD, 1), lambda i, pt: (i // tiles_per_seq, 0, 0, 0))
    specs = ([row(D), row(FOX_W), row(GLA_V_W)] + [full(a) for a in consts]
             + [hbm, hbm, hbm, col, col, col, pl.BlockSpec((None, H, 1), lambda i, pt: (i // tiles_per_seq, 0, 0)),
                full(wsuf)])
    return pl.pallas_call(
        functools.partial(_finish_decode_body, G=G, NP=NP, RING=DECODE_RING),
        grid_spec=pltpu.PrefetchScalarGridSpec(
            num_scalar_prefetch=1,
            grid=(n_tiles,),
            in_specs=specs,
            out_specs=[row(D), col],
            scratch_shapes=[pltpu.VMEM((DECODE_RING, G, H, HD, PAGE), F32),
                            pltpu.VMEM((DECODE_RING, G, H, HD, PAGE), F32),
                            pltpu.VMEM((DECODE_RING, G, H, PAGE), F32),
                            pltpu.SemaphoreType.DMA((3, DECODE_RING)),
                            pltpu.VMEM((H, HD, PAGE), F32), pltpu.VMEM((H, HD, PAGE), F32),
                            pltpu.VMEM((H, PAGE), F32), pltpu.VMEM((H, 1, PAGE), F32),
                            pltpu.VMEM((H, 1, PAGE), F32)]),
        out_shape=[jax.ShapeDtypeStruct((N, D), F32), jax.ShapeDtypeStruct((DB, H, HD, 1), F32)],
        compiler_params=_params(1),
        name="finish_decode",
    )(page_table, x, o_fox, o_gla, *consts, kc, vc, fc, q, kn, vn, cn, wsuf)


def _layer_weights(l, g_pre_mix, w_in, b_f, w_alpha_up, b_alpha, g_gla_norm, w_fox_out, w_gla_out, w_o,
                   g_post_mix, g_pre_mlp, w_up, w_down, g_post_mlp):
    sizes = (FOX_W, FOX_W, FOX_W, FOX_HEADS, GLA_K_W, GLA_K_W, GLA_V_W, GLA_V_W, GLA_RANK)
    off = [0]
    for s in sizes:
        off.append(off[-1] + s)
    w = w_in[l]
    D = w.shape[0]
    wb = w.astype(BF16)
    wqkv = wb[:, :off[3]]
    wgla = wb[:, off[4]:off[7]]
    wog = wb[:, off[7]:off[8]]
    wgf = wb[:, off[9]:off[9] + D]
    wgg = wb[:, off[9] + D:off[9] + 2 * D]
    nf = F_PARTS * FOX_HEADS
    pad = LANES - nf - GLA_RANK
    wsm = jnp.concatenate([wb[:, off[3]:off[4]]] * F_PARTS + [wb[:, off[8]:off[9]], jnp.zeros((D, pad), BF16)], axis=1)
    wau = jnp.concatenate([jnp.zeros((nf, GLA_K_W), BF16), w_alpha_up[l].astype(BF16),
                           jnp.zeros((pad, GLA_K_W), BF16)], axis=0)
    bfr = jnp.concatenate([b_f[l]] * F_PARTS + [jnp.zeros((LANES - nf,), F32)])[None, :]
    r = lambda a: a[l][None, :]
    return dict(g1=r(g_pre_mix), wqkv=wqkv, wgla=wgla, wsm=wsm, wau=wau, bal=r(b_alpha), bfr=bfr,
                gn=r(g_gla_norm), wog=wog, wgf=wgf, wgg=wgg, wfo=w_fox_out[l].astype(BF16),
                wgo=w_gla_out[l].astype(BF16), wo=w_o[l].astype(BF16), g2=r(g_post_mix), g3=r(g_pre_mlp),
                wu=w_up[l].astype(BF16), wd=w_down[l].astype(BF16), g4=r(g_post_mlp))


def kernel(x_prompt, x_sample, cache_k, cache_v, cache_logf, state_gla, page_table, g_pre_mix, w_in, b_f, w_alpha_up, b_alpha, g_gla_norm, w_fox_out, w_gla_out, w_o, g_post_mix, g_pre_mlp, w_up, w_down, g_post_mlp):
    B, S, D = x_prompt.shape
    DB = x_sample.shape[0]
    depth = w_in.shape[0]
    assert depth == 1 and x_sample.shape[1] == 1
    W = _layer_weights(0, g_pre_mix, w_in, b_f, w_alpha_up, b_alpha, g_gla_norm, w_fox_out, w_gla_out, w_o,
                       g_post_mix, g_pre_mlp, w_up, w_down, g_post_mlp)
    proj = lambda x, tm, sc: _in_proj(x, W["g1"], W["wqkv"], W["wgla"], W["wsm"], W["wau"], W["bal"], W["bfr"], tm, sc)

    PADT = LANES
    xs = jnp.zeros((1, PADT, D), F32).at[0, :DB].set(x_sample[:, 0])
    kT, vT, _, qTb, _, _, lfT, qg_s, kg_s, vg_s, la_s = proj(xs, PADT, 1.0)
    k_s = kT[0].T[:DB].reshape(DB, FOX_HEADS, FOX_HD)
    v_s = vT[0].T[:DB].reshape(DB, FOX_HEADS, FOX_HD)
    lf_s = lfT[0].T[:DB]
    q_s = qTb[0].T[:DB].astype(F32).reshape(DB, FOX_HEADS, FOX_HD)
    kc = jnp.transpose(cache_k, (0, 1, 3, 4, 2))
    vc = jnp.transpose(cache_v, (0, 1, 3, 4, 2))
    fc = jnp.transpose(cache_logf, (0, 1, 3, 2))

    kT, vT, vTb, qTb, kb, fa, lfT, qg, kg, vg, la = proj(x_prompt, ROW_TILE, LOG2E)
    o_fox = _fox_prompt(kb, fa, qTb, vTb, ATT_TILE)
    o_gla, s_p = _gla(qg, kg, vg, la, jnp.zeros((B, GLA_HEADS, GLA_DK, GLA_DV), F32), B)
    y_p, o_fs = _finish_decode(x_prompt.reshape(B * S, D), o_fox.reshape(B * S, FOX_W), o_gla.reshape(B * S, GLA_V_W),
                               W, ROW_TILE, kc, vc, fc, page_table, q_s[..., None], k_s[..., None], v_s[..., None],
                               lf_s[..., None], PAGES_PER_STEP)
    y_p = y_p.reshape(B, S, D)
    to_tok = lambda a: jnp.transpose(a.reshape(1, B, FOX_HEADS, FOX_HD, S), (0, 1, 4, 2, 3))
    k_p, v_p = to_tok(kT), to_tok(vT)
    lf_p = jnp.transpose(lfT, (0, 2, 1))[None]

    o_fox_s = jnp.zeros((PADT, FOX_W), BF16).at[:DB].set(o_fs.reshape(DB, FOX_W).astype(BF16))
    C = GLA_CHUNK
    pad_c = lambda a: jnp.zeros((DB, C, a.shape[-1]), a.dtype).at[:, 0].set(a[0, :DB])
    o_gs, s_s = _gla(pad_c(qg_s), pad_c(kg_s), pad_c(vg_s), pad_c(la_s), state_gla[0], DB // 2)
    o_gla_s = jnp.zeros((PADT, GLA_V_W), F32).at[:DB].set(o_gs[:, 0])
    y_s = _finish(xs[0], o_fox_s, o_gla_s, W, PADT)[:DB].reshape(DB, 1, D)

    return (y_p, y_s, k_p, v_p, lf_p, s_p[None],
            k_s[None, :, None], v_s[None, :, None], lf_s[None, :, None], s_s[None])
```

```python
import functools
import math

import jax
import jax.numpy as jnp
from jax import lax
from jax.experimental import pallas as pl
from jax.experimental.pallas import tpu as pltpu

F32 = jnp.float32
BF16 = jnp.bfloat16

FOX_HEADS = 8
FOX_HD = 64
FOX_W = FOX_HEADS * FOX_HD
GLA_HEADS = 4
GLA_DK = 64
GLA_DV = 128
GLA_K_W = GLA_HEADS * GLA_DK
GLA_V_W = GLA_HEADS * GLA_DV
GLA_RANK = 16
GLA_GATE_NORM = 16.0
GLA_CHUNK = 64
EPS = 1e-6
LANES = 128
VMEM_LIMIT = 56 * 1024 * 1024
LOG2E = 1.4426950408889634
F_PARTS = 3

ROW_TILE = 512
ATT_TILE = 256
PAGES_PER_STEP = 8
DECODE_RING = 2
CUM_BLOCK = 128
GLA_INTERLEAVE = 8


def _params(n_axes, flags=None):
    return pltpu.CompilerParams(dimension_semantics=("arbitrary",) * n_axes,
                                vmem_limit_bytes=VMEM_LIMIT, flags=flags)


def _log_sigmoid(z):
    return jnp.minimum(z, 0.0) - jnp.log1p(jnp.exp(-jnp.abs(z)))


def _sigmoid(z):
    return 1.0 / (1.0 + jnp.exp(-z))


def _rms(x, g):
    return x * lax.rsqrt(jnp.mean(x * x, axis=-1, keepdims=True) + EPS) * g


def _dot(a, b):
    return jnp.dot(a, b, preferred_element_type=F32)


def _dot_nt(a, b):
    return lax.dot_general(a, b, (((1,), (1,)), ((), ())), preferred_element_type=F32)


def _split3(x):
    hi = x.astype(BF16)
    r1 = x - hi.astype(F32)
    mid = r1.astype(BF16)
    lo = (r1 - mid.astype(F32)).astype(BF16)
    return hi, mid, lo


def _in_proj_body(x_ref, g_ref, wqkv_ref, wgla_ref, wsm_ref, wau_ref, bal_ref, bf_ref, tri_ref,
                  kT_ref, vT_ref, vTb_ref, qTb_ref, kb_ref, fa_ref, lfT_ref, qg_ref, kg_ref, vg_ref, la_ref,
                  car_ref, *, logit_scale):
    @pl.when(pl.program_id(1) == 0)
    def _():
        car_ref[...] = jnp.zeros_like(car_ref)

    h = _rms(x_ref[...], g_ref[...]).astype(BF16)
    qkv = _dot(h, wqkv_ref[...])
    q = qkv[:, :FOX_W] * (FOX_HD ** -0.5 * logit_scale)
    k = qkv[:, FOX_W:2 * FOX_W]
    v = qkv[:, 2 * FOX_W:]
    qTb_ref[...] = q.T.astype(BF16)
    kb_ref[...] = k.astype(BF16)
    kT_ref[...] = k.T
    vT = v.T
    vT_ref[...] = vT
    vTb_ref[...] = vT.astype(BF16)
    gl = _dot(h, wgla_ref[...])
    qg_ref[...] = gl[:, :GLA_K_W] * (GLA_DK ** -0.5)
    kg_ref[...] = gl[:, GLA_K_W:2 * GLA_K_W]
    vg_ref[...] = gl[:, 2 * GLA_K_W:].astype(BF16)
    sm = _dot(h, wsm_ref[...])
    lf = _log_sigmoid(sm + bf_ref[...])
    lfT_ref[...] = lf.T[:FOX_HEADS, :]
    z = _dot(sm.astype(BF16), wau_ref[...]) + bal_ref[...]
    la_ref[...] = _log_sigmoid(z) * (1.0 / GLA_GATE_NORM)
    tri = tri_ref[...]
    parts = _split3(lf)
    run = car_ref[...]
    blocks = []
    for r0 in range(0, lf.shape[0], CUM_BLOCK):
        blocks.append(run + sum(_dot(tri, part[r0:r0 + CUM_BLOCK]) for part in parts))
        run = blocks[-1][CUM_BLOCK - 1:, :]
    F = jnp.concatenate(blocks, axis=0)
    car_ref[...] = run
    hi, mid, lo = _split3(F * logit_scale)
    lane = lax.broadcasted_iota(jnp.int32, F.shape, 1)
    zero = jnp.zeros_like(hi)
    fa_ref[...] = jnp.where(lane < FOX_HEADS, hi,
                            jnp.where(lane < 2 * FOX_HEADS, mid, jnp.where(lane < 3 * FOX_HEADS, lo, zero)))


def _in_proj(x, g, wqkv, wgla, wsm, wau, bal, bfr, tm, logit_scale):
    B, S, D = x.shape
    nt = S // tm
    assert tm % CUM_BLOCK == 0
    tri = (lax.broadcasted_iota(jnp.int32, (CUM_BLOCK, CUM_BLOCK), 1)
           <= lax.broadcasted_iota(jnp.int32, (CUM_BLOCK, CUM_BLOCK), 0)).astype(BF16)
    row = lambda w: pl.BlockSpec((None, tm, w), lambda b, i: (b, i, 0))
    col = lambda w: pl.BlockSpec((None, w, tm), lambda b, i: (b, 0, i))
    full = lambda a: pl.BlockSpec(a.shape, lambda b, i: (0,) * a.ndim)
    sds = jax.ShapeDtypeStruct
    return pl.pallas_call(
        functools.partial(_in_proj_body, logit_scale=logit_scale),
        grid=(B, nt),
        in_specs=[row(D), full(g), full(wqkv), full(wgla), full(wsm), full(wau), full(bal), full(bfr), full(tri)],
        out_specs=[col(FOX_W), col(FOX_W), col(FOX_W), col(FOX_W), row(FOX_W), row(LANES), col(FOX_HEADS),
                   row(GLA_K_W), row(GLA_K_W), row(GLA_V_W), row(GLA_K_W)],
        out_shape=[sds((B, FOX_W, S), F32), sds((B, FOX_W, S), F32), sds((B, FOX_W, S), BF16),
                   sds((B, FOX_W, S), BF16), sds((B, S, FOX_W), BF16), sds((B, S, LANES), BF16),
                   sds((B, FOX_HEADS, S), F32),
                   sds((B, S, GLA_K_W), F32), sds((B, S, GLA_K_W), F32), sds((B, S, GLA_V_W), BF16),
                   sds((B, S, GLA_K_W), F32)],
        scratch_shapes=[pltpu.VMEM((1, LANES), F32)],
        compiler_params=_params(2),
        name="in_proj",
    )(x, g, wqkv, wgla, wsm, wau, bal, bfr, tri)


def _fox_body(k_ref, fa_ref, qT_ref, vT_ref, o_ref, rhs_ref, acc_ref, m_ref, *, t):
    i = pl.program_id(1)
    npair = FOX_HEADS // 2
    ONES = 16

    @pl.when((pl.program_id(0) == 0) & (i == 0))
    def _():
        rr = lax.broadcasted_iota(jnp.int32, (LANES, 2 * t), 0)
        cc = lax.broadcasted_iota(jnp.int32, (LANES, 2 * t), 1)
        head = rr & (FOX_HEADS - 1)
        for p in range(npair):
            mine = head == jnp.where(cc < t, 2 * p, 2 * p + 1)
            neg = jnp.where(rr < F_PARTS * FOX_HEADS, jnp.where(mine, -1.0, 0.0), 0.0)
            rhs_ref[p, LANES:, :] = neg.astype(BF16)
            rhs_ref[p, :FOX_HD, t:] = jnp.zeros((FOX_HD, t), BF16)
            rhs_ref[p, FOX_HD:LANES, :t] = jnp.zeros((FOX_HD, t), BF16)

    for p in range(npair):
        rhs_ref[p, :FOX_HD, :t] = qT_ref[p * LANES:p * LANES + FOX_HD, :]
        rhs_ref[p, FOX_HD:LANES, t:] = qT_ref[p * LANES + FOX_HD:(p + 1) * LANES, :]
    m_ref[...] = jnp.full(m_ref.shape, -jnp.inf, F32)
    acc_ref[...] = jnp.zeros(acc_ref.shape, F32)
    ones = jnp.where(lax.broadcasted_iota(jnp.int32, (ONES, t), 0) == 0, 1.0, 0.0).astype(BF16)

    def run(blocks):
        work = []
        for j, diagonal in blocks:
            c0 = pl.multiple_of(j * t, t)
            fa = fa_ref[pl.ds(c0, t), :]
            work += [(c0, fa, diagonal, p) for p in range(npair)]

        def scores(c0, fa, diagonal, p):
            lhs = jnp.concatenate([k_ref[pl.ds(c0, t), p * LANES:(p + 1) * LANES], fa], axis=1)
            sT = _dot(lhs, rhs_ref[p])
            if diagonal:
                key = lax.broadcasted_iota(jnp.int32, (t, 2 * t), 0)
                qry = lax.broadcasted_iota(jnp.int32, (t, 2 * t), 1)
                sT = jnp.where(key <= jnp.where(qry >= t, qry - t, qry), sT, -jnp.inf)
            return sT

        def soft(p, sT):
            m_old = m_ref[p]
            m_new = jnp.maximum(m_old, jnp.max(sT, axis=0, keepdims=True))
            m_ref[p] = m_new
            return jnp.exp2(m_old - m_new), jnp.exp2(sT - m_new).astype(BF16)

        def pv(c0, p, alpha, pT):
            for e in range(2):
                r0 = p * LANES + e * FOX_HD
                v1 = jnp.concatenate([vT_ref[r0:r0 + FOX_HD, pl.ds(c0, t)], ones], axis=0)
                cols = slice(e * t, (e + 1) * t)
                acc_ref[p, e] = alpha[:, cols] * acc_ref[p, e] + _dot(v1, pT[:, cols])

        n = len(work)
        sT = {x: scores(*work[x]) for x in range(2)}
        done = {}
        for x in range(n):
            done[x] = soft(work[x][3], sT.pop(x))
            if x + 2 < n:
                sT[x + 2] = scores(*work[x + 2])
            if x >= 1:
                pv(work[x - 1][0], work[x - 1][3], *done.pop(x - 1))
        pv(work[n - 1][0], work[n - 1][3], *done.pop(n - 1))

    def two_blocks(jj, carry):
        run([(2 * jj, False), (2 * jj + 1, False)])
        return carry

    lax.fori_loop(0, lax.shift_right_logical(i, 1), two_blocks, 0)

    @pl.when((i & 1) == 1)
    def _():
        run([(i - 1, False), (i, True)])

    @pl.when((i & 1) == 0)
    def _():
        run([(i, True)])

    for p in range(npair):
        halves = []
        for e in range(2):
            acc = acc_ref[p, e]
            halves.append(acc[:FOX_HD] * (1.0 / acc[FOX_HD:FOX_HD + 1]))
        o_ref[:, p * LANES:(p + 1) * LANES] = jnp.concatenate(halves, axis=0).T.astype(o_ref.dtype)


def _fox_prompt(kb, fa, qTb, vTb, t):
    B, S, _ = kb.shape
    npair = FOX_HEADS // 2
    return pl.pallas_call(
        functools.partial(_fox_body, t=t),
        grid=(B, S // t),
        in_specs=[pl.BlockSpec((None, S, FOX_W), lambda b, i: (b, 0, 0)),
                  pl.BlockSpec((None, S, LANES), lambda b, i: (b, 0, 0)),
                  pl.BlockSpec((None, FOX_W, t), lambda b, i: (b, 0, i)),
                  pl.BlockSpec((None, FOX_W, S), lambda b, i: (b, 0, 0))],
        out_specs=pl.BlockSpec((None, t, FOX_W), lambda b, i: (b, i, 0)),
        out_shape=jax.ShapeDtypeStruct((B, S, FOX_W), BF16),
        scratch_shapes=[pltpu.VMEM((npair, 2 * LANES, 2 * t), BF16), pltpu.VMEM((npair, 2, FOX_HD + 16, t), F32),
                        pltpu.VMEM((npair, 1, 2 * t), F32)],
        compiler_params=_params(2),
        name="fox_prompt",
    )(kb, fa, qTb, vTb)


def _gla_body(qg_ref, kg_ref, vg_ref, la_ref, s0_ref, o_ref, sN_ref, st_ref, *, bt, C):
    c = pl.program_id(1)
    KW, VW = GLA_K_W, GLA_V_W
    blk = (lax.broadcasted_iota(jnp.int32, (VW, KW), 0) // GLA_DV
           == lax.broadcasted_iota(jnp.int32, (VW, KW), 1) // GLA_DK)

    @pl.when(c == 0)
    def _():
        for b in range(bt):
            s0 = s0_ref[b]
            rows = [jnp.concatenate([s0[h] if hh == h else jnp.zeros((GLA_DK, GLA_DV), F32)
                                     for hh in range(GLA_HEADS)], axis=1) for h in range(GLA_HEADS)]
            st_ref[b] = jnp.concatenate(rows, axis=0).T

    tri = (lax.broadcasted_iota(jnp.int32, (C, C), 1) <= lax.broadcasted_iota(jnp.int32, (C, C), 0))
    tri_b = tri.astype(BF16)
    tri4 = jnp.concatenate([tri] * GLA_HEADS, axis=0)
    head_of_lane = lax.broadcasted_iota(jnp.int32, (C, KW), 1) // GLA_DK

    def chain(b):
        bc = sum(_dot(tri_b, part) for part in _split3(la_ref[b]))
        yield
        ref = bc[C // 2 - 1:C // 2, :]
        last = bc[C - 1:C, :]
        q = qg_ref[b]
        k = kg_ref[b]
        v = vg_ref[b]
        st = st_ref[b]
        qi = (q * jnp.exp(bc)).astype(BF16)
        qt = q * jnp.exp(bc - ref)
        kt = (k * jnp.exp(ref - bc)).astype(BF16)
        kd = (k * jnp.exp(last - bc)).astype(BF16)
        qm = jnp.concatenate([jnp.where(head_of_lane == h, qt, 0.0) for h in range(GLA_HEADS)],
                             axis=0).astype(BF16)
        a_raw = _dot_nt(qm, kt)
        inter = _dot_nt(qi, st.astype(BF16))
        upd = _dot(v.astype(F32).T.astype(BF16), kd)
        yield
        A = jnp.where(tri4, a_raw, 0.0).astype(BF16)
        intra = jnp.concatenate([_dot(A[h * C:(h + 1) * C], v[:, h * GLA_DV:(h + 1) * GLA_DV])
                                 for h in range(GLA_HEADS)], axis=1)
        yield
        o_ref[b] = inter + intra
        st_ref[b] = jnp.where(blk, st * jnp.exp(last) + upd, 0.0)
        yield

    width = math.gcd(bt, GLA_INTERLEAVE)

    def group(i, _):
        chains = [chain(i * width + r) for r in range(width)]
        for _stage in range(4):
            for ch in chains:
                next(ch)
        return 0

    lax.fori_loop(0, bt // width, group, 0)

    @pl.when(c == pl.num_programs(1) - 1)
    def _():
        for b in range(bt):
            s = st_ref[b].T
            for h in range(GLA_HEADS):
                sN_ref[b, h] = s[h * GLA_DK:(h + 1) * GLA_DK, h * GLA_DV:(h + 1) * GLA_DV]


def _gla(qg, kg, vg, la, s0, bt):
    B, S, _ = qg.shape
    C = GLA_CHUNK
    spec = lambda w: pl.BlockSpec((bt, C, w), lambda g, c: (g, c, 0))
    sspec = pl.BlockSpec((bt, GLA_HEADS, GLA_DK, GLA_DV), lambda g, c: (g, 0, 0, 0))
    return pl.pallas_call(
        functools.partial(_gla_body, bt=bt, C=C),
        grid=(B // bt, S // C),
        in_specs=[spec(GLA_K_W), spec(GLA_K_W), spec(GLA_V_W), spec(GLA_K_W), sspec],
        out_specs=[spec(GLA_V_W), sspec],
        out_shape=[jax.ShapeDtypeStruct((B, S, GLA_V_W), F32),
                   jax.ShapeDtypeStruct((B, GLA_HEADS, GLA_DK, GLA_DV), F32)],
        scratch_shapes=[pltpu.VMEM((bt, GLA_V_W, GLA_K_W), F32)],
        compiler_params=_params(2),
        name="gla_scan",
    )(qg, kg, vg, la, s0)


DECODE_SUBSTEPS = 8
MLP_CHUNKS = 4


def _finish_stages(x_ref, of_ref, og_ref, g1_ref, gn_ref, g2_ref, g3_ref, g4_ref, wog_ref, wgf_ref, wgg_ref, wfo_ref,
                   wgo_ref, wo_ref, wu_ref, wd_ref, y_ref):
    x = x_ref[...]
    h = _rms(x, g1_ref[...]).astype(BF16)
    gate = _dot(h, wog_ref[...])
    gf = _dot(h, wgf_ref[...])
    yield
    gg = _dot(h, wgg_ref[...])
    fo = _dot(of_ref[...], wfo_ref[...])
    o = og_ref[...]
    ys = []
    for hh in range(GLA_HEADS):
        sl = slice(hh * GLA_DV, (hh + 1) * GLA_DV)
        gt = gate[:, sl]
        ys.append(_rms(o[:, sl], gn_ref[...]) * (gt * _sigmoid(gt)))
    y_gla = jnp.concatenate(ys, axis=1).astype(BF16)
    yield
    u = _sigmoid(gf) * fo + _sigmoid(gg) * _dot(y_gla, wgo_ref[...])
    x1 = x + _rms(_dot(u.astype(BF16), wo_ref[...]), g2_ref[...])
    h2 = _rms(x1, g3_ref[...]).astype(BF16)
    yield
    ff = wu_ref.shape[1] // MLP_CHUNKS
    acc = jnp.zeros(x.shape, F32)
    for c in range(MLP_CHUNKS):
        up = jnp.maximum(_dot(h2, wu_ref[:, c * ff:(c + 1) * ff]), 0.0)
        if c == MLP_CHUNKS - 1:
            yield
        acc = acc + _dot((up * up).astype(BF16), wd_ref[c * ff:(c + 1) * ff, :])
        if c < MLP_CHUNKS - 1:
            yield
    y_ref[...] = x1 + _rms(acc, g4_ref[...])
    yield


def _finish_body(*refs):
    for _ in _finish_stages(*refs):
        pass


def _finish_decode_body(pt_ref, *refs, G, NP, RING):
    fin, dec = refs[:16], refs[16:22]
    y_ref, o_ref = refs[22:24]
    scratch = refs[24:]
    i = pl.program_id(0)
    n_sub = pl.num_programs(0) * DECODE_SUBSTEPS
    stages = _finish_stages(*fin, y_ref)
    for r in range(DECODE_SUBSTEPS):
        next(stages)
        _decode_step(pt_ref, dec, o_ref, scratch, i * DECODE_SUBSTEPS + r, n_sub, r, DECODE_SUBSTEPS, G, NP, RING)


def _finish(x, o_fox, o_gla, W, tm):
    N, D = x.shape
    row = lambda w: pl.BlockSpec((tm, w), lambda i: (i, 0))
    full = lambda a: pl.BlockSpec(a.shape, lambda i: (0,) * a.ndim, pipeline_mode=pl.Buffered(1))
    consts = [W[n] for n in ("g1", "gn", "g2", "g3", "g4", "wog", "wgf", "wgg", "wfo", "wgo", "wo", "wu", "wd")]
    return pl.pallas_call(
        _finish_body,
        grid=(N // tm,),
        in_specs=[row(D), row(FOX_W), row(GLA_V_W)] + [full(a) for a in consts],
        out_specs=row(D),
        out_shape=jax.ShapeDtypeStruct((N, D), F32),
        compiler_params=_params(1),
        name="finish",
    )(x, o_fox, o_gla, *consts)


def _decode_step(pt_ref, dec, o_ref, scratch, step, n_steps, r, period, G, NP, RING):
    kc_hbm, vc_hbm, fc_hbm, qkv_ref, cn_ref, w_ref = dec
    kbuf, vbuf, fbuf, sem, qs_ref, acc_ref, car_ref, m_ref, l_ref = scratch
    ng = NP // G
    g = lax.rem(step, jnp.int32(ng))
    _, _, H, HD, PAGE = kbuf.shape
    reachable = lambda gval: (gval - r) % math.gcd(ng, period) == 0

    def page_copies(s, slot):
        bb, gg = (s // ng, s % ng) if isinstance(s, int) else (lax.div(s, jnp.int32(ng)), lax.rem(s, jnp.int32(ng)))
        out = []
        for j in range(G):
            page = pt_ref[bb, NP - 1 - (gg * G + j)]
            out.append(pltpu.make_async_copy(kc_hbm.at[0, page], kbuf.at[slot, j], sem.at[0, slot]))
            out.append(pltpu.make_async_copy(vc_hbm.at[0, page], vbuf.at[slot, j], sem.at[1, slot]))
            out.append(pltpu.make_async_copy(fc_hbm.at[0, page], fbuf.at[slot, j], sem.at[2, slot]))
        return out

    if r == 0:
        @pl.when(step == 0)
        def _():
            for s in range(RING - 1):
                for cp in page_copies(s, s):
                    cp.start()

    ahead = step + (RING - 1)
    for cp in page_copies(jnp.minimum(ahead, n_steps - 1), lax.rem(ahead, jnp.int32(RING))):
        cp.start()

    slot = lax.rem(step, jnp.int32(RING))
    for cp in page_copies(step, slot):
        cp.wait()
    k_refs = [kbuf.at[slot, j] for j in range(G)]
    v_refs = [vbuf.at[slot, j] for j in range(G)]
    f_refs = [fbuf.at[slot, j] for j in range(G)]

    if reachable(0):
        @pl.when(g == 0)
        def _():
            qs_ref[...] = jnp.broadcast_to(qkv_ref[:, :, 0:1], (H, HD, PAGE))
            car_ref[...] = jnp.broadcast_to(cn_ref[...], (H, PAGE))
            m_ref[...] = jnp.full(m_ref.shape, -jnp.inf, F32)
            l_ref[...] = jnp.zeros(l_ref.shape, F32)
            acc_ref[...] = jnp.zeros(acc_ref.shape, F32)

    lf = jnp.concatenate([f_refs[j][...] for j in range(G)], axis=0)
    w = w_ref[...]
    suf = sum(_dot(part, w) for part in _split3(lf))
    carry = car_ref[...]
    bias = []
    for j in range(G):
        bias.append(suf[j * H:(j + 1) * H, :PAGE] + carry)
        carry = carry + suf[j * H:(j + 1) * H, PAGE:]
    car_ref[...] = carry

    for h in range(H):
        qh = qs_ref[h]
        rows = [jnp.sum(k_refs[j][h] * qh, axis=0, keepdims=True) + bias[j][h:h + 1, :] for j in range(G)]
        m_old = m_ref[h]
        m_new = jnp.maximum(m_old, functools.reduce(jnp.maximum, rows))
        alpha = jnp.exp(m_old - m_new)
        a = acc_ref[h] * alpha
        ls = l_ref[h] * alpha
        for j in range(G):
            pr = jnp.exp(rows[j] - m_new)
            ls = ls + pr
            a = a + v_refs[j][h] * pr
        acc_ref[h] = a
        l_ref[h] = ls
        m_ref[h] = m_new

    if reachable(ng - 1):
        @pl.when(g == ng - 1)
        def _():
            for h in range(H):
                new = qkv_ref[h]
                ln = jnp.sum(qs_ref[h] * new[:, 1:2], axis=0, keepdims=True)
                m = m_ref[h]
                mf = jnp.maximum(jnp.max(m, axis=1, keepdims=True), ln)
                wl = jnp.exp(m - mf)
                pn = jnp.exp(ln - mf)[:, 0:1]
                den = jnp.sum(l_ref[h] * wl, axis=1, keepdims=True) + pn
                num = jnp.sum(acc_ref[h] * wl, axis=1, keepdims=True) + pn * new[:, 2:3]
                o_ref[h] = num / den

    if r >= period - (RING - 1):
        @pl.when(step >= n_steps - (RING - 1))
        def _():
            for cp in page_copies(n_steps - 1, lax.rem(ahead, jnp.int32(RING))):
                cp.wait()


def _finish_decode(x, o_fox, o_gla, W, tm, kc, vc, fc, page_table, qkv, cn, G):
    N, D = x.shape
    DB, NP = page_table.shape
    _, _, H, HD, PAGE = kc.shape
    ng = NP // G
    n_tiles = N // tm
    assert ng % DECODE_SUBSTEPS == 0 and n_tiles * DECODE_SUBSTEPS == DB * ng
    tiles_per_seq = ng // DECODE_SUBSTEPS
    pos = lax.broadcasted_iota(jnp.int32, (PAGE, 2 * PAGE), 0)
    lane = lax.broadcasted_iota(jnp.int32, (PAGE, 2 * PAGE), 1)
    wsuf = jnp.where(lane < PAGE, pos > lane, True).astype(BF16)

    row = lambda w: pl.BlockSpec((tm, w), lambda i, pt: (i, 0))
    full = lambda a: pl.BlockSpec(a.shape, lambda i, pt: (0,) * a.ndim, pipeline_mode=pl.Buffered(1))
    consts = [W[n] for n in ("g1", "gn", "g2", "g3", "g4", "wog", "wgf", "wgg", "wfo", "wgo", "wo", "wu", "wd")]
    hbm = pl.BlockSpec(memory_space=pl.ANY)
    col = pl.BlockSpec((None, H, HD, 1), lambda i, pt: (i // tiles_per_seq, 0, 0, 0))
    specs = ([row(D), row(FOX_W), row(GLA_V_W)] + [full(a) for a in consts]
             + [hbm, hbm, hbm, pl.BlockSpec((None, H, HD, 3), lambda i, pt: (i // tiles_per_seq, 0, 0, 0)),
                pl.BlockSpec((None, H, 1), lambda i, pt: (i // tiles_per_seq, 0, 0)), full(wsuf)])
    return pl.pallas_call(
        functools.partial(_finish_decode_body, G=G, NP=NP, RING=DECODE_RING),
        grid_spec=pltpu.PrefetchScalarGridSpec(
            num_scalar_prefetch=1,
            grid=(n_tiles,),
            in_specs=specs,
            out_specs=[row(D), col],
            scratch_shapes=[pltpu.VMEM((DECODE_RING, G, H, HD, PAGE), F32),
                            pltpu.VMEM((DECODE_RING, G, H, HD, PAGE), F32),
                            pltpu.VMEM((DECODE_RING, G, H, PAGE), F32),
                            pltpu.SemaphoreType.DMA((3, DECODE_RING)),
                            pltpu.VMEM((H, HD, PAGE), F32), pltpu.VMEM((H, HD, PAGE), F32),
                            pltpu.VMEM((H, PAGE), F32), pltpu.VMEM((H, 1, PAGE), F32),
                            pltpu.VMEM((H, 1, PAGE), F32)]),
        out_shape=[jax.ShapeDtypeStruct((N, D), F32), jax.ShapeDtypeStruct((DB, H, HD, 1), F32)],
        compiler_params=_params(1),
        name="finish_decode",
    )(page_table, x, o_fox, o_gla, *consts, kc, vc, fc, qkv, cn, wsuf)


def _layer_weights(l, g_pre_mix, w_in, b_f, w_alpha_up, b_alpha, g_gla_norm, w_fox_out, w_gla_out, w_o,
                   g_post_mix, g_pre_mlp, w_up, w_down, g_post_mlp):
    sizes = (FOX_W, FOX_W, FOX_W, FOX_HEADS, GLA_K_W, GLA_K_W, GLA_V_W, GLA_V_W, GLA_RANK)
    off = [0]
    for s in sizes:
        off.append(off[-1] + s)
    w = w_in[l]
    D = w.shape[0]
    wb = w.astype(BF16)
    wqkv = wb[:, :off[3]]
    wgla = wb[:, off[4]:off[7]]
    wog = wb[:, off[7]:off[8]]
    wgf = wb[:, off[9]:off[9] + D]
    wgg = wb[:, off[9] + D:off[9] + 2 * D]
    nf = F_PARTS * FOX_HEADS
    pad = LANES - nf - GLA_RANK
    wsm = jnp.concatenate([wb[:, off[3]:off[4]]] * F_PARTS + [wb[:, off[8]:off[9]], jnp.zeros((D, pad), BF16)], axis=1)
    wau = jnp.concatenate([jnp.zeros((nf, GLA_K_W), BF16), w_alpha_up[l].astype(BF16),
                           jnp.zeros((pad, GLA_K_W), BF16)], axis=0)
    bfr = jnp.concatenate([b_f[l]] * F_PARTS + [jnp.zeros((LANES - nf,), F32)])[None, :]
    r = lambda a: a[l][None, :]
    return dict(g1=r(g_pre_mix), wqkv=wqkv, wgla=wgla, wsm=wsm, wau=wau, bal=r(b_alpha), bfr=bfr,
                gn=r(g_gla_norm), wog=wog, wgf=wgf, wgg=wgg, wfo=w_fox_out[l].astype(BF16),
                wgo=w_gla_out[l].astype(BF16), wo=w_o[l].astype(BF16), g2=r(g_post_mix), g3=r(g_pre_mlp),
                wu=w_up[l].astype(BF16), wd=w_down[l].astype(BF16), g4=r(g_post_mlp))


def kernel(x_prompt, x_sample, cache_k, cache_v, cache_logf, state_gla, page_table, g_pre_mix, w_in, b_f, w_alpha_up, b_alpha, g_gla_norm, w_fox_out, w_gla_out, w_o, g_post_mix, g_pre_mlp, w_up, w_down, g_post_mlp):
    B, S, D = x_prompt.shape
    DB = x_sample.shape[0]
    depth = w_in.shape[0]
    assert depth == 1 and x_sample.shape[1] == 1
    W = _layer_weights(0, g_pre_mix, w_in, b_f, w_alpha_up, b_alpha, g_gla_norm, w_fox_out, w_gla_out, w_o,
                       g_post_mix, g_pre_mlp, w_up, w_down, g_post_mlp)
    proj = lambda x, tm, sc: _in_proj(x, W["g1"], W["wqkv"], W["wgla"], W["wsm"], W["wau"], W["bal"], W["bfr"], tm, sc)

    PADT = LANES
    xs = jnp.zeros((1, PADT, D), F32).at[0, :DB].set(x_sample[:, 0])
    kT, vT, _, qTb, _, _, lfT, qg_s, kg_s, vg_s, la_s = proj(xs, PADT, 1.0)
    k_s = kT[0].T[:DB].reshape(DB, FOX_HEADS, FOX_HD)
    v_s = vT[0].T[:DB].reshape(DB, FOX_HEADS, FOX_HD)
    lf_s = lfT[0].T[:DB]
    q_s = qTb[0].T[:DB].astype(F32).reshape(DB, FOX_HEADS, FOX_HD)
    kc = jnp.transpose(cache_k, (0, 1, 3, 4, 2))
    vc = jnp.transpose(cache_v, (0, 1, 3, 4, 2))
    fc = jnp.transpose(cache_logf, (0, 1, 3, 2))

    kT, vT, vTb, qTb, kb, fa, lfT, qg, kg, vg, la = proj(x_prompt, ROW_TILE, LOG2E)
    o_fox = _fox_prompt(kb, fa, qTb, vTb, ATT_TILE)
    o_gla, s_p = _gla(qg, kg, vg, la, jnp.zeros((B, GLA_HEADS, GLA_DK, GLA_DV), F32), B)
    y_p, o_fs = _finish_decode(x_prompt.reshape(B * S, D), o_fox.reshape(B * S, FOX_W), o_gla.reshape(B * S, GLA_V_W),
                               W, ROW_TILE, kc, vc, fc, page_table, jnp.stack([q_s, k_s, v_s], axis=-1),
                               lf_s[..., None], PAGES_PER_STEP)
    y_p = y_p.reshape(B, S, D)
    to_tok = lambda a: jnp.transpose(a.reshape(1, B, FOX_HEADS, FOX_HD, S), (0, 1, 4, 2, 3))
    k_p, v_p = to_tok(kT), to_tok(vT)
    lf_p = jnp.transpose(lfT, (0, 2, 1))[None]

    o_fox_s = jnp.zeros((PADT, FOX_W), BF16).at[:DB].set(o_fs.reshape(DB, FOX_W).astype(BF16))
    C = GLA_CHUNK
    pad_c = lambda a: jnp.zeros((DB, C, a.shape[-1]), a.dtype).at[:, 0].set(a[0, :DB])
    o_gs, s_s = _gla(pad_c(qg_s), pad_c(kg_s), pad_c(vg_s), pad_c(la_s), state_gla[0], DB // 2)
    o_gla_s = jnp.zeros((PADT, GLA_V_W), F32).at[:DB].set(o_gs[:, 0])
    y_s = _finish(xs[0], o_fox_s, o_gla_s, W, PADT)[:DB].reshape(DB, 1, D)

    return (y_p, y_s, k_p, v_p, lf_p, s_p[None],
            k_s[None, :, None], v_s[None, :, None], lf_s[None, :, None], s_s[None])
```

```python
import functools
import math

import jax
import jax.numpy as jnp
from jax import lax
from jax.experimental import pallas as pl
from jax.experimental.pallas import tpu as pltpu

F32 = jnp.float32
BF16 = jnp.bfloat16

FOX_HEADS = 8
FOX_HD = 64
FOX_W = FOX_HEADS * FOX_HD
GLA_HEADS = 4
GLA_DK = 64
GLA_DV = 128
GLA_K_W = GLA_HEADS * GLA_DK
GLA_V_W = GLA_HEADS * GLA_DV
GLA_RANK = 16
GLA_GATE_NORM = 16.0
GLA_CHUNK = 64
EPS = 1e-6
LANES = 128
VMEM_LIMIT = 56 * 1024 * 1024
LOG2E = 1.4426950408889634
F_PARTS = 3

ROW_TILE = 512
ATT_TILE = 256
PAGES_PER_STEP = 8
DECODE_RING = 2
CUM_BLOCK = 128
GLA_INTERLEAVE = 8


def _params(n_axes, flags=None):
    return pltpu.CompilerParams(dimension_semantics=("arbitrary",) * n_axes,
                                vmem_limit_bytes=VMEM_LIMIT, flags=flags)


def _log_sigmoid(z):
    return jnp.minimum(z, 0.0) - jnp.log1p(jnp.exp(-jnp.abs(z)))


def _sigmoid(z):
    return 1.0 / (1.0 + jnp.exp(-z))


def _rms(x, g):
    return x * lax.rsqrt(jnp.mean(x * x, axis=-1, keepdims=True) + EPS) * g


def _dot(a, b):
    return jnp.dot(a, b, preferred_element_type=F32)


def _dot_nt(a, b):
    return lax.dot_general(a, b, (((1,), (1,)), ((), ())), preferred_element_type=F32)


def _split3(x):
    hi = x.astype(BF16)
    r1 = x - hi.astype(F32)
    mid = r1.astype(BF16)
    lo = (r1 - mid.astype(F32)).astype(BF16)
    return hi, mid, lo


def _in_proj_body(x_ref, g_ref, wqkv_ref, wgla_ref, wsm_ref, wau_ref, bal_ref, bf_ref, tri_ref,
                  kT_ref, vT_ref, vTb_ref, qTb_ref, kb_ref, fa_ref, lfT_ref, qg_ref, kg_ref, vg_ref, la_ref,
                  car_ref, *, logit_scale):
    @pl.when(pl.program_id(1) == 0)
    def _():
        car_ref[...] = jnp.zeros_like(car_ref)

    h = _rms(x_ref[...], g_ref[...]).astype(BF16)
    sm = _dot(h, wsm_ref[...])
    qkv = _dot(h, wqkv_ref[...])
    lf = _log_sigmoid(sm + bf_ref[...])
    lfT_ref[...] = lf.T[:FOX_HEADS, :]
    z = _dot(sm.astype(BF16), wau_ref[...]) + bal_ref[...]
    la_ref[...] = _log_sigmoid(z) * (1.0 / GLA_GATE_NORM)
    tri = tri_ref[...]
    parts = _split3(lf)
    run = car_ref[...]
    blocks = []
    for r0 in range(0, lf.shape[0], CUM_BLOCK):
        blocks.append(run + sum(_dot(tri, part[r0:r0 + CUM_BLOCK]) for part in parts))
        run = blocks[-1][CUM_BLOCK - 1:, :]
    F = jnp.concatenate(blocks, axis=0)
    car_ref[...] = run
    hi, mid, lo = _split3(F * logit_scale)
    lane = lax.broadcasted_iota(jnp.int32, F.shape, 1)
    zero = jnp.zeros_like(hi)
    fa_ref[...] = jnp.where(lane < FOX_HEADS, hi,
                            jnp.where(lane < 2 * FOX_HEADS, mid, jnp.where(lane < 3 * FOX_HEADS, lo, zero)))
    gl = _dot(h, wgla_ref[...])
    q = qkv[:, :FOX_W] * (FOX_HD ** -0.5 * logit_scale)
    k = qkv[:, FOX_W:2 * FOX_W]
    v = qkv[:, 2 * FOX_W:]
    qTb_ref[...] = q.T.astype(BF16)
    kb_ref[...] = k.astype(BF16)
    kT_ref[...] = k.T
    vT = v.T
    vT_ref[...] = vT
    vTb_ref[...] = vT.astype(BF16)
    qg_ref[...] = gl[:, :GLA_K_W] * (GLA_DK ** -0.5)
    kg_ref[...] = gl[:, GLA_K_W:2 * GLA_K_W]
    vg_ref[...] = gl[:, 2 * GLA_K_W:].astype(BF16)


def _in_proj(x, g, wqkv, wgla, wsm, wau, bal, bfr, tm, logit_scale):
    B, S, D = x.shape
    nt = S // tm
    assert tm % CUM_BLOCK == 0
    tri = (lax.broadcasted_iota(jnp.int32, (CUM_BLOCK, CUM_BLOCK), 1)
           <= lax.broadcasted_iota(jnp.int32, (CUM_BLOCK, CUM_BLOCK), 0)).astype(BF16)
    row = lambda w: pl.BlockSpec((None, tm, w), lambda b, i: (b, i, 0))
    col = lambda w: pl.BlockSpec((None, w, tm), lambda b, i: (b, 0, i))
    full = lambda a: pl.BlockSpec(a.shape, lambda b, i: (0,) * a.ndim)
    sds = jax.ShapeDtypeStruct
    return pl.pallas_call(
        functools.partial(_in_proj_body, logit_scale=logit_scale),
        grid=(B, nt),
        in_specs=[row(D), full(g), full(wqkv), full(wgla), full(wsm), full(wau), full(bal), full(bfr), full(tri)],
        out_specs=[col(FOX_W), col(FOX_W), col(FOX_W), col(FOX_W), row(FOX_W), row(LANES), col(FOX_HEADS),
                   row(GLA_K_W), row(GLA_K_W), row(GLA_V_W), row(GLA_K_W)],
        out_shape=[sds((B, FOX_W, S), F32), sds((B, FOX_W, S), F32), sds((B, FOX_W, S), BF16),
                   sds((B, FOX_W, S), BF16), sds((B, S, FOX_W), BF16), sds((B, S, LANES), BF16),
                   sds((B, FOX_HEADS, S), F32),
                   sds((B, S, GLA_K_W), F32), sds((B, S, GLA_K_W), F32), sds((B, S, GLA_V_W), BF16),
                   sds((B, S, GLA_K_W), F32)],
        scratch_shapes=[pltpu.VMEM((1, LANES), F32)],
        compiler_params=_params(2),
        name="in_proj",
    )(x, g, wqkv, wgla, wsm, wau, bal, bfr, tri)


def _fox_body(k_ref, fa_ref, qT_ref, vT_ref, o_ref, rhs_ref, acc_ref, m_ref, *, t):
    i = pl.program_id(1)
    npair = FOX_HEADS // 2
    ONES = 16

    @pl.when((pl.program_id(0) == 0) & (i == 0))
    def _():
        rr = lax.broadcasted_iota(jnp.int32, (LANES, 2 * t), 0)
        cc = lax.broadcasted_iota(jnp.int32, (LANES, 2 * t), 1)
        head = rr & (FOX_HEADS - 1)
        for p in range(npair):
            mine = head == jnp.where(cc < t, 2 * p, 2 * p + 1)
            neg = jnp.where(rr < F_PARTS * FOX_HEADS, jnp.where(mine, -1.0, 0.0), 0.0)
            rhs_ref[p, LANES:, :] = neg.astype(BF16)
            rhs_ref[p, :FOX_HD, t:] = jnp.zeros((FOX_HD, t), BF16)
            rhs_ref[p, FOX_HD:LANES, :t] = jnp.zeros((FOX_HD, t), BF16)

    for p in range(npair):
        rhs_ref[p, :FOX_HD, :t] = qT_ref[p * LANES:p * LANES + FOX_HD, :]
        rhs_ref[p, FOX_HD:LANES, t:] = qT_ref[p * LANES + FOX_HD:(p + 1) * LANES, :]
    m_ref[...] = jnp.full(m_ref.shape, -jnp.inf, F32)
    acc_ref[...] = jnp.zeros(acc_ref.shape, F32)
    ones = jnp.where(lax.broadcasted_iota(jnp.int32, (ONES, t), 0) == 0, 1.0, 0.0).astype(BF16)

    def run(blocks):
        work = []
        for j, diagonal in blocks:
            c0 = pl.multiple_of(j * t, t)
            fa = fa_ref[pl.ds(c0, t), :]
            work += [(c0, fa, diagonal, p) for p in range(npair)]

        def scores(c0, fa, diagonal, p):
            lhs = jnp.concatenate([k_ref[pl.ds(c0, t), p * LANES:(p + 1) * LANES], fa], axis=1)
            sT = _dot(lhs, rhs_ref[p])
            if diagonal:
                key = lax.broadcasted_iota(jnp.int32, (t, 2 * t), 0)
                qry = lax.broadcasted_iota(jnp.int32, (t, 2 * t), 1)
                sT = jnp.where(key <= jnp.where(qry >= t, qry - t, qry), sT, -jnp.inf)
            return sT

        def soft(p, sT):
            m_old = m_ref[p]
            m_new = jnp.maximum(m_old, jnp.max(sT, axis=0, keepdims=True))
            m_ref[p] = m_new
            return jnp.exp2(m_old - m_new), jnp.exp2(sT - m_new).astype(BF16)

        def pv(c0, p, alpha, pT):
            for e in range(2):
                r0 = p * LANES + e * FOX_HD
                v1 = jnp.concatenate([vT_ref[r0:r0 + FOX_HD, pl.ds(c0, t)], ones], axis=0)
                cols = slice(e * t, (e + 1) * t)
                acc_ref[p, e] = alpha[:, cols] * acc_ref[p, e] + _dot(v1, pT[:, cols])

        n = len(work)
        sT = {x: scores(*work[x]) for x in range(2)}
        done = {}
        for x in range(n):
            done[x] = soft(work[x][3], sT.pop(x))
            if x + 2 < n:
                sT[x + 2] = scores(*work[x + 2])
            if x >= 1:
                pv(work[x - 1][0], work[x - 1][3], *done.pop(x - 1))
        pv(work[n - 1][0], work[n - 1][3], *done.pop(n - 1))

    def two_blocks(jj, carry):
        run([(2 * jj, False), (2 * jj + 1, False)])
        return carry

    lax.fori_loop(0, lax.shift_right_logical(i, 1), two_blocks, 0)

    @pl.when((i & 1) == 1)
    def _():
        run([(i - 1, False), (i, True)])

    @pl.when((i & 1) == 0)
    def _():
        run([(i, True)])

    for p in range(npair):
        halves = []
        for e in range(2):
            acc = acc_ref[p, e]
            halves.append(acc[:FOX_HD] * (1.0 / acc[FOX_HD:FOX_HD + 1]))
        o_ref[:, p * LANES:(p + 1) * LANES] = jnp.concatenate(halves, axis=0).T.astype(o_ref.dtype)


def _fox_prompt(kb, fa, qTb, vTb, t):
    B, S, _ = kb.shape
    npair = FOX_HEADS // 2
    return pl.pallas_call(
        functools.partial(_fox_body, t=t),
        grid=(B, S // t),
        in_specs=[pl.BlockSpec((None, S, FOX_W), lambda b, i: (b, 0, 0)),
                  pl.BlockSpec((None, S, LANES), lambda b, i: (b, 0, 0)),
                  pl.BlockSpec((None, FOX_W, t), lambda b, i: (b, 0, i)),
                  pl.BlockSpec((None, FOX_W, S), lambda b, i: (b, 0, 0))],
        out_specs=pl.BlockSpec((None, t, FOX_W), lambda b, i: (b, i, 0)),
        out_shape=jax.ShapeDtypeStruct((B, S, FOX_W), BF16),
        scratch_shapes=[pltpu.VMEM((npair, 2 * LANES, 2 * t), BF16), pltpu.VMEM((npair, 2, FOX_HD + 16, t), F32),
                        pltpu.VMEM((npair, 1, 2 * t), F32)],
        compiler_params=_params(2),
        name="fox_prompt",
    )(kb, fa, qTb, vTb)


def _gla_body(qg_ref, kg_ref, vg_ref, la_ref, s0_ref, o_ref, sN_ref, st_ref, *, bt, C):
    c = pl.program_id(1)
    KW, VW = GLA_K_W, GLA_V_W
    blk = (lax.broadcasted_iota(jnp.int32, (VW, KW), 0) // GLA_DV
           == lax.broadcasted_iota(jnp.int32, (VW, KW), 1) // GLA_DK)

    @pl.when(c == 0)
    def _():
        for b in range(bt):
            s0 = s0_ref[b]
            rows = [jnp.concatenate([s0[h] if hh == h else jnp.zeros((GLA_DK, GLA_DV), F32)
                                     for hh in range(GLA_HEADS)], axis=1) for h in range(GLA_HEADS)]
            st_ref[b] = jnp.concatenate(rows, axis=0).T

    tri = (lax.broadcasted_iota(jnp.int32, (C, C), 1) <= lax.broadcasted_iota(jnp.int32, (C, C), 0))
    tri_b = tri.astype(BF16)
    tri4 = jnp.concatenate([tri] * GLA_HEADS, axis=0)
    head_of_lane = lax.broadcasted_iota(jnp.int32, (C, KW), 1) // GLA_DK

    def chain(b):
        bc = sum(_dot(tri_b, part) for part in _split3(la_ref[b]))
        yield
        ref = bc[C // 2 - 1:C // 2, :]
        last = bc[C - 1:C, :]
        q = qg_ref[b]
        k = kg_ref[b]
        v = vg_ref[b]
        st = st_ref[b]
        qi = (q * jnp.exp(bc)).astype(BF16)
        qt = q * jnp.exp(bc - ref)
        kt = (k * jnp.exp(ref - bc)).astype(BF16)
        kd = (k * jnp.exp(last - bc)).astype(BF16)
        qm = jnp.concatenate([jnp.where(head_of_lane == h, qt, 0.0) for h in range(GLA_HEADS)],
                             axis=0).astype(BF16)
        a_raw = _dot_nt(qm, kt)
        inter = _dot_nt(qi, st.astype(BF16))
        upd = _dot(v.astype(F32).T.astype(BF16), kd)
        yield
        A = jnp.where(tri4, a_raw, 0.0).astype(BF16)
        intra = jnp.concatenate([_dot(A[h * C:(h + 1) * C], v[:, h * GLA_DV:(h + 1) * GLA_DV])
                                 for h in range(GLA_HEADS)], axis=1)
        yield
        o_ref[b] = inter + intra
        st_ref[b] = jnp.where(blk, st * jnp.exp(last) + upd, 0.0)
        yield

    width = math.gcd(bt, GLA_INTERLEAVE)

    def group(i, _):
        chains = [chain(i * width + r) for r in range(width)]
        for _stage in range(4):
            for ch in chains:
                next(ch)
        return 0

    lax.fori_loop(0, bt // width, group, 0)

    @pl.when(c == pl.num_programs(1) - 1)
    def _():
        for b in range(bt):
            s = st_ref[b].T
            for h in range(GLA_HEADS):
                sN_ref[b, h] = s[h * GLA_DK:(h + 1) * GLA_DK, h * GLA_DV:(h + 1) * GLA_DV]


def _gla(qg, kg, vg, la, s0, bt):
    B, S, _ = qg.shape
    C = GLA_CHUNK
    spec = lambda w: pl.BlockSpec((bt, C, w), lambda g, c: (g, c, 0))
    sspec = pl.BlockSpec((bt, GLA_HEADS, GLA_DK, GLA_DV), lambda g, c: (g, 0, 0, 0))
    return pl.pallas_call(
        functools.partial(_gla_body, bt=bt, C=C),
        grid=(B // bt, S // C),
        in_specs=[spec(GLA_K_W), spec(GLA_K_W), spec(GLA_V_W), spec(GLA_K_W), sspec],
        out_specs=[spec(GLA_V_W), sspec],
        out_shape=[jax.ShapeDtypeStruct((B, S, GLA_V_W), F32),
                   jax.ShapeDtypeStruct((B, GLA_HEADS, GLA_DK, GLA_DV), F32)],
        scratch_shapes=[pltpu.VMEM((bt, GLA_V_W, GLA_K_W), F32)],
        compiler_params=_params(2),
        name="gla_scan",
    )(qg, kg, vg, la, s0)


DECODE_SUBSTEPS = 8
MLP_CHUNKS = 4


def _finish_stages(x_ref, of_ref, og_ref, g1_ref, gn_ref, g2_ref, g3_ref, g4_ref, wog_ref, wgf_ref, wgg_ref, wfo_ref,
                   wgo_ref, wo_ref, wu_ref, wd_ref, y_ref):
    x = x_ref[...]
    h = _rms(x, g1_ref[...]).astype(BF16)
    gate = _dot(h, wog_ref[...])
    gf = _dot(h, wgf_ref[...])
    yield
    gg = _dot(h, wgg_ref[...])
    fo = _dot(of_ref[...], wfo_ref[...])
    o = og_ref[...]
    ys = []
    for hh in range(GLA_HEADS):
        sl = slice(hh * GLA_DV, (hh + 1) * GLA_DV)
        gt = gate[:, sl]
        ys.append(_rms(o[:, sl], gn_ref[...]) * (gt * _sigmoid(gt)))
    y_gla = jnp.concatenate(ys, axis=1).astype(BF16)
    yield
    u = _sigmoid(gf) * fo + _sigmoid(gg) * _dot(y_gla, wgo_ref[...])
    x1 = x + _rms(_dot(u.astype(BF16), wo_ref[...]), g2_ref[...])
    h2 = _rms(x1, g3_ref[...]).astype(BF16)
    yield
    ff = wu_ref.shape[1] // MLP_CHUNKS
    acc = jnp.zeros(x.shape, F32)
    for c in range(MLP_CHUNKS):
        up = jnp.maximum(_dot(h2, wu_ref[:, c * ff:(c + 1) * ff]), 0.0)
        if c == MLP_CHUNKS - 1:
            yield
        acc = acc + _dot((up * up).astype(BF16), wd_ref[c * ff:(c + 1) * ff, :])
        if c < MLP_CHUNKS - 1:
            yield
    y_ref[...] = x1 + _rms(acc, g4_ref[...])
    yield


def _finish_body(*refs):
    for _ in _finish_stages(*refs):
        pass


def _finish_decode_body(pt_ref, *refs, G, NP, RING):
    fin, dec = refs[:16], refs[16:24]
    y_ref, o_ref = refs[24:26]
    scratch = refs[26:]
    i = pl.program_id(0)
    n_sub = pl.num_programs(0) * DECODE_SUBSTEPS
    stages = _finish_stages(*fin, y_ref)
    for r in range(DECODE_SUBSTEPS):
        next(stages)
        _decode_step(pt_ref, dec, o_ref, scratch, i * DECODE_SUBSTEPS + r, n_sub, r, DECODE_SUBSTEPS, G, NP, RING)


def _finish(x, o_fox, o_gla, W, tm):
    N, D = x.shape
    row = lambda w: pl.BlockSpec((tm, w), lambda i: (i, 0))
    full = lambda a: pl.BlockSpec(a.shape, lambda i: (0,) * a.ndim, pipeline_mode=pl.Buffered(1))
    consts = [W[n] for n in ("g1", "gn", "g2", "g3", "g4", "wog", "wgf", "wgg", "wfo", "wgo", "wo", "wu", "wd")]
    return pl.pallas_call(
        _finish_body,
        grid=(N // tm,),
        in_specs=[row(D), row(FOX_W), row(GLA_V_W)] + [full(a) for a in consts],
        out_specs=row(D),
        out_shape=jax.ShapeDtypeStruct((N, D), F32),
        compiler_params=_params(1),
        name="finish",
    )(x, o_fox, o_gla, *consts)


def _decode_step(pt_ref, dec, o_ref, scratch, step, n_steps, r, period, G, NP, RING):
    kc_hbm, vc_hbm, fc_hbm, q_ref, kn_ref, vn_ref, cn_ref, w_ref = dec
    kbuf, vbuf, fbuf, sem, qs_ref, acc_ref, car_ref, m_ref, l_ref = scratch
    ng = NP // G
    g = lax.rem(step, jnp.int32(ng))
    _, _, H, HD, PAGE = kbuf.shape
    reachable = lambda gval: (gval - r) % math.gcd(ng, period) == 0

    def page_copies(s, slot):
        bb, gg = (s // ng, s % ng) if isinstance(s, int) else (lax.div(s, jnp.int32(ng)), lax.rem(s, jnp.int32(ng)))
        out = []
        for j in range(G):
            page = pt_ref[bb, NP - 1 - (gg * G + j)]
            out.append(pltpu.make_async_copy(kc_hbm.at[0, page], kbuf.at[slot, j], sem.at[0, slot]))
            out.append(pltpu.make_async_copy(vc_hbm.at[0, page], vbuf.at[slot, j], sem.at[1, slot]))
            out.append(pltpu.make_async_copy(fc_hbm.at[0, page], fbuf.at[slot, j], sem.at[2, slot]))
        return out

    if r == 0:
        @pl.when(step == 0)
        def _():
            for s in range(RING - 1):
                for cp in page_copies(s, s):
                    cp.start()

    ahead = step + (RING - 1)
    for cp in page_copies(jnp.minimum(ahead, n_steps - 1), lax.rem(ahead, jnp.int32(RING))):
        cp.start()

    slot = lax.rem(step, jnp.int32(RING))
    for cp in page_copies(step, slot):
        cp.wait()
    k_refs = [kbuf.at[slot, j] for j in range(G)]
    v_refs = [vbuf.at[slot, j] for j in range(G)]
    f_refs = [fbuf.at[slot, j] for j in range(G)]

    if reachable(0):
        @pl.when(g == 0)
        def _():
            qs_ref[...] = jnp.broadcast_to(q_ref[...], (H, HD, PAGE))
            car_ref[...] = jnp.broadcast_to(cn_ref[...], (H, PAGE))
            m_ref[...] = jnp.full(m_ref.shape, -jnp.inf, F32)
            l_ref[...] = jnp.zeros(l_ref.shape, F32)
            acc_ref[...] = jnp.zeros(acc_ref.shape, F32)

    lf = jnp.concatenate([f_refs[j][...] for j in range(G)], axis=0)
    w = w_ref[...]
    suf = sum(_dot(part, w) for part in _split3(lf))
    carry = car_ref[...]
    bias = []
    for j in range(G):
        bias.append(suf[j * H:(j + 1) * H, :PAGE] + carry)
        carry = carry + suf[j * H:(j + 1) * H, PAGE:]
    car_ref[...] = carry

    for h in range(H):
        qh = qs_ref[h]
        rows = [jnp.sum(k_refs[j][h] * qh, axis=0, keepdims=True) + bias[j][h:h + 1, :] for j in range(G)]
        m_old = m_ref[h]
        m_new = jnp.maximum(m_old, functools.reduce(jnp.maximum, rows))
        alpha = jnp.exp(m_old - m_new)
        a = acc_ref[h] * alpha
        ls = l_ref[h] * alpha
        for j in range(G):
            pr = jnp.exp(rows[j] - m_new)
            ls = ls + pr
            a = a + v_refs[j][h] * pr
        acc_ref[h] = a
        l_ref[h] = ls
        m_ref[h] = m_new

    if reachable(ng - 1):
        @pl.when(g == ng - 1)
        def _():
            for h in range(H):
                ln = jnp.sum(qs_ref[h] * kn_ref[h], axis=0, keepdims=True)
                m = m_ref[h]
                mf = jnp.maximum(jnp.max(m, axis=1, keepdims=True), ln)
                wl = jnp.exp(m - mf)
                pn = jnp.exp(ln - mf)[:, 0:1]
                den = jnp.sum(l_ref[h] * wl, axis=1, keepdims=True) + pn
                num = jnp.sum(acc_ref[h] * wl, axis=1, keepdims=True) + pn * vn_ref[h]
                o_ref[h] = num / den

    if r >= period - (RING - 1):
        @pl.when(step >= n_steps - (RING - 1))
        def _():
            for cp in page_copies(n_steps - 1, lax.rem(ahead, jnp.int32(RING))):
                cp.wait()


def _finish_decode(x, o_fox, o_gla, W, tm, kc, vc, fc, page_table, q, kn, vn, cn, G):
    N, D = x.shape
    DB, NP = page_table.shape
    _, _, H, HD, PAGE = kc.shape
    ng = NP // G
    n_tiles = N // tm
    assert ng % DECODE_SUBSTEPS == 0 and n_tiles * DECODE_SUBSTEPS == DB * ng
    tiles_per_seq = ng // DECODE_SUBSTEPS
    pos = lax.broadcasted_iota(jnp.int32, (PAGE, 2 * PAGE), 0)
    lane = lax.broadcasted_iota(jnp.int32, (PAGE, 2 * PAGE), 1)
    wsuf = jnp.where(lane < PAGE, pos > lane, True).astype(BF16)

    row = lambda w: pl.BlockSpec((tm, w), lambda i, pt: (i, 0))
    full = lambda a: pl.BlockSpec(a.shape, lambda i, pt: (0,) * a.ndim, pipeline_mode=pl.Buffered(1))
    consts = [W[n] for n in ("g1", "gn", "g2", "g3", "g4", "wog", "wgf", "wgg", "wfo", "wgo", "wo", "wu", "wd")]
    hbm = pl.BlockSpec(memory_space=pl.ANY)
    col = pl.BlockSpec((None, H, HD, 1), lambda i, pt: (i // tiles_per_seq, 0, 0, 0))
    specs = ([row(D), row(FOX_W), row(GLA_V_W)] + [full(a) for a in consts]
             + [hbm, hbm, hbm, col, col, col, pl.BlockSpec((None, H, 1), lambda i, pt: (i // tiles_per_seq, 0, 0)),
                full(wsuf)])
    return pl.pallas_call(
        functools.partial(_finish_decode_body, G=G, NP=NP, RING=DECODE_RING),
        grid_spec=pltpu.PrefetchScalarGridSpec(
            num_scalar_prefetch=1,
            grid=(n_tiles,),
            in_specs=specs,
            out_specs=[row(D), col],
            scratch_shapes=[pltpu.VMEM((DECODE_RING, G, H, HD, PAGE), F32),
                            pltpu.VMEM((DECODE_RING, G, H, HD, PAGE), F32),
                            pltpu.VMEM((DECODE_RING, G, H, PAGE), F32),
                            pltpu.SemaphoreType.DMA((3, DECODE_RING)),
                            pltpu.VMEM((H, HD, PAGE), F32), pltpu.VMEM((H, HD, PAGE), F32),
                            pltpu.VMEM((H, PAGE), F32), pltpu.VMEM((H, 1, PAGE), F32),
                            pltpu.VMEM((H, 1, PAGE), F32)]),
        out_shape=[jax.ShapeDtypeStruct((N, D), F32), jax.ShapeDtypeStruct((DB, H, HD, 1), F32)],
        compiler_params=_params(1),
        name="finish_decode",
    )(page_table, x, o_fox, o_gla, *consts, kc, vc, fc, q, kn, vn, cn, wsuf)


def _layer_weights(l, g_pre_mix, w_in, b_f, w_alpha_up, b_alpha, g_gla_norm, w_fox_out, w_gla_out, w_o,
                   g_post_mix, g_pre_mlp, w_up, w_down, g_post_mlp):
    sizes = (FOX_W, FOX_W, FOX_W, FOX_HEADS, GLA_K_W, GLA_K_W, GLA_V_W, GLA_V_W, GLA_RANK)
    off = [0]
    for s in sizes:
        off.append(off[-1] + s)
    w = w_in[l]
    D = w.shape[0]
    wb = w.astype(BF16)
    wqkv = wb[:, :off[3]]
    wgla = wb[:, off[4]:off[7]]
    wog = wb[:, off[7]:off[8]]
    wgf = wb[:, off[9]:off[9] + D]
    wgg = wb[:, off[9] + D:off[9] + 2 * D]
    nf = F_PARTS * FOX_HEADS
    pad = LANES - nf - GLA_RANK
    wsm = jnp.concatenate([wb[:, off[3]:off[4]]] * F_PARTS + [wb[:, off[8]:off[9]], jnp.zeros((D, pad), BF16)], axis=1)
    wau = jnp.concatenate([jnp.zeros((nf, GLA_K_W), BF16), w_alpha_up[l].astype(BF16),
                           jnp.zeros((pad, GLA_K_W), BF16)], axis=0)
    bfr = jnp.concatenate([b_f[l]] * F_PARTS + [jnp.zeros((LANES - nf,), F32)])[None, :]
    r = lambda a: a[l][None, :]
    return dict(g1=r(g_pre_mix), wqkv=wqkv, wgla=wgla, wsm=wsm, wau=wau, bal=r(b_alpha), bfr=bfr,
                gn=r(g_gla_norm), wog=wog, wgf=wgf, wgg=wgg, wfo=w_fox_out[l].astype(BF16),
                wgo=w_gla_out[l].astype(BF16), wo=w_o[l].astype(BF16), g2=r(g_post_mix), g3=r(g_pre_mlp),
                wu=w_up[l].astype(BF16), wd=w_down[l].astype(BF16), g4=r(g_post_mlp))


def kernel(x_prompt, x_sample, cache_k, cache_v, cache_logf, state_gla, page_table, g_pre_mix, w_in, b_f, w_alpha_up, b_alpha, g_gla_norm, w_fox_out, w_gla_out, w_o, g_post_mix, g_pre_mlp, w_up, w_down, g_post_mlp):
    B, S, D = x_prompt.shape
    DB = x_sample.shape[0]
    depth = w_in.shape[0]
    assert depth == 1 and x_sample.shape[1] == 1
    W = _layer_weights(0, g_pre_mix, w_in, b_f, w_alpha_up, b_alpha, g_gla_norm, w_fox_out, w_gla_out, w_o,
                       g_post_mix, g_pre_mlp, w_up, w_down, g_post_mlp)
    proj = lambda x, tm, sc: _in_proj(x, W["g1"], W["wqkv"], W["wgla"], W["wsm"], W["wau"], W["bal"], W["bfr"], tm, sc)

    PADT = LANES
    xs = jnp.zeros((1, PADT, D), F32).at[0, :DB].set(x_sample[:, 0])
    kT, vT, _, qTb, _, _, lfT, qg_s, kg_s, vg_s, la_s = proj(xs, PADT, 1.0)
    k_s = kT[0].T[:DB].reshape(DB, FOX_HEADS, FOX_HD)
    v_s = vT[0].T[:DB].reshape(DB, FOX_HEADS, FOX_HD)
    lf_s = lfT[0].T[:DB]
    q_s = qTb[0].T[:DB].astype(F32).reshape(DB, FOX_HEADS, FOX_HD)
    kc = jnp.transpose(cache_k, (0, 1, 3, 4, 2))
    vc = jnp.transpose(cache_v, (0, 1, 3, 4, 2))
    fc = jnp.transpose(cache_logf, (0, 1, 3, 2))

    kT, vT, vTb, qTb, kb, fa, lfT, qg, kg, vg, la = proj(x_prompt, ROW_TILE, LOG2E)
    o_fox = _fox_prompt(kb, fa, qTb, vTb, ATT_TILE)
    o_gla, s_p = _gla(qg, kg, vg, la, jnp.zeros((B, GLA_HEADS, GLA_DK, GLA_DV), F32), B)
    y_p, o_fs = _finish_decode(x_prompt.reshape(B * S, D), o_fox.reshape(B * S, FOX_W), o_gla.reshape(B * S, GLA_V_W),
                               W, ROW_TILE, kc, vc, fc, page_table, q_s[..., None], k_s[..., None], v_s[..., None],
                               lf_s[..., None], PAGES_PER_STEP)
    y_p = y_p.reshape(B, S, D)
    to_tok = lambda a: jnp.transpose(a.reshape(1, B, FOX_HEADS, FOX_HD, S), (0, 1, 4, 2, 3))
    k_p, v_p = to_tok(kT), to_tok(vT)
    lf_p = jnp.transpose(lfT, (0, 2, 1))[None]

    o_fox_s = jnp.zeros((PADT, FOX_W), BF16).at[:DB].set(o_fs.reshape(DB, FOX_W).astype(BF16))
    C = GLA_CHUNK
    pad_c = lambda a: jnp.zeros((DB, C, a.shape[-1]), a.dtype).at[:, 0].set(a[0, :DB])
    o_gs, s_s = _gla(pad_c(qg_s), pad_c(kg_s), pad_c(vg_s), pad_c(la_s), state_gla[0], DB // 2)
    o_gla_s = jnp.zeros((PADT, GLA_V_W), F32).at[:DB].set(o_gs[:, 0])
    y_s = _finish(xs[0], o_fox_s, o_gla_s, W, PADT)[:DB].reshape(DB, 1, D)

    return (y_p, y_s, k_p, v_p, lf_p, s_p[None],
            k_s[None, :, None], v_s[None, :, None], lf_s[None, :, None], s_s[None])
```

```python
import functools
import math

import jax
import jax.numpy as jnp
from jax import lax
from jax.experimental import pallas as pl
from jax.experimental.pallas import tpu as pltpu

F32 = jnp.float32
BF16 = jnp.bfloat16

FOX_HEADS = 8
FOX_HD = 64
FOX_W = FOX_HEADS * FOX_HD
GLA_HEADS = 4
GLA_DK = 64
GLA_DV = 128
GLA_K_W = GLA_HEADS * GLA_DK
GLA_V_W = GLA_HEADS * GLA_DV
GLA_RANK = 16
GLA_GATE_NORM = 16.0
GLA_CHUNK = 64
EPS = 1e-6
LANES = 128
VMEM_LIMIT = 56 * 1024 * 1024
LOG2E = 1.4426950408889634
F_PARTS = 3

ROW_TILE = 512
ATT_TILE = 256
PAGES_PER_STEP = 8
DECODE_RING = 2
CUM_BLOCK = 128
GLA_INTERLEAVE = 8


def _params(n_axes, flags=None):
    return pltpu.CompilerParams(dimension_semantics=("arbitrary",) * n_axes,
                                vmem_limit_bytes=VMEM_LIMIT, flags=flags)


def _log_sigmoid(z):
    return jnp.minimum(z, 0.0) - jnp.log1p(jnp.exp(-jnp.abs(z)))


def _sigmoid(z):
    return 1.0 / (1.0 + jnp.exp(-z))


def _rms(x, g):
    return x * lax.rsqrt(jnp.mean(x * x, axis=-1, keepdims=True) + EPS) * g


def _dot(a, b):
    return jnp.dot(a, b, preferred_element_type=F32)


def _dot_nt(a, b):
    return lax.dot_general(a, b, (((1,), (1,)), ((), ())), preferred_element_type=F32)


def _split3(x):
    hi = x.astype(BF16)
    r1 = x - hi.astype(F32)
    mid = r1.astype(BF16)
    lo = (r1 - mid.astype(F32)).astype(BF16)
    return hi, mid, lo


def _in_proj_body(x_ref, g_ref, wqkv_ref, wgla_ref, wsm_ref, wau_ref, bal_ref, bf_ref, tri_ref,
                  kT_ref, vT_ref, vTb_ref, qTb_ref, kb_ref, fa_ref, lfT_ref, qg_ref, kg_ref, vg_ref, la_ref,
                  car_ref, *, logit_scale):
    @pl.when(pl.program_id(1) == 0)
    def _():
        car_ref[...] = jnp.zeros_like(car_ref)

    h = _rms(x_ref[...], g_ref[...]).astype(BF16)
    sm = _dot(h, wsm_ref[...])
    qkv = _dot(h, wqkv_ref[...])
    lf = _log_sigmoid(sm + bf_ref[...])
    lfT_ref[...] = lf.T[:FOX_HEADS, :]
    z = _dot(sm.astype(BF16), wau_ref[...]) + bal_ref[...]
    la_ref[...] = _log_sigmoid(z) * (1.0 / GLA_GATE_NORM)
    tri = tri_ref[...]
    parts = _split3(lf)
    run = car_ref[...]
    blocks = []
    for r0 in range(0, lf.shape[0], CUM_BLOCK):
        blocks.append(run + sum(_dot(tri, part[r0:r0 + CUM_BLOCK]) for part in parts))
        run = blocks[-1][CUM_BLOCK - 1:, :]
    F = jnp.concatenate(blocks, axis=0)
    car_ref[...] = run
    hi, mid, lo = _split3(F * logit_scale)
    lane = lax.broadcasted_iota(jnp.int32, F.shape, 1)
    zero = jnp.zeros_like(hi)
    fa_ref[...] = jnp.where(lane < FOX_HEADS, hi,
                            jnp.where(lane < 2 * FOX_HEADS, mid, jnp.where(lane < 3 * FOX_HEADS, lo, zero)))
    gl = _dot(h, wgla_ref[...])
    q = qkv[:, :FOX_W] * (FOX_HD ** -0.5 * logit_scale)
    k = qkv[:, FOX_W:2 * FOX_W]
    v = qkv[:, 2 * FOX_W:]
    qTb_ref[...] = q.T.astype(BF16)
    kb_ref[...] = k.astype(BF16)
    kT_ref[...] = k.T
    vT = v.T
    vT_ref[...] = vT
    vTb_ref[...] = vT.astype(BF16)
    qg_ref[...] = gl[:, :GLA_K_W] * (GLA_DK ** -0.5)
    kg_ref[...] = gl[:, GLA_K_W:2 * GLA_K_W]
    vg_ref[...] = gl[:, 2 * GLA_K_W:].astype(BF16)


def _in_proj(x, g, wqkv, wgla, wsm, wau, bal, bfr, tm, logit_scale):
    B, S, D = x.shape
    nt = S // tm
    assert tm % CUM_BLOCK == 0
    tri = (lax.broadcasted_iota(jnp.int32, (CUM_BLOCK, CUM_BLOCK), 1)
           <= lax.broadcasted_iota(jnp.int32, (CUM_BLOCK, CUM_BLOCK), 0)).astype(BF16)
    row = lambda w: pl.BlockSpec((None, tm, w), lambda b, i: (b, i, 0))
    col = lambda w: pl.BlockSpec((None, w, tm), lambda b, i: (b, 0, i))
    full = lambda a: pl.BlockSpec(a.shape, lambda b, i: (0,) * a.ndim)
    sds = jax.ShapeDtypeStruct
    return pl.pallas_call(
        functools.partial(_in_proj_body, logit_scale=logit_scale),
        grid=(B, nt),
        in_specs=[row(D), full(g), full(wqkv), full(wgla), full(wsm), full(wau), full(bal), full(bfr), full(tri)],
        out_specs=[col(FOX_W), col(FOX_W), col(FOX_W), col(FOX_W), row(FOX_W), row(LANES), col(FOX_HEADS),
                   row(GLA_K_W), row(GLA_K_W), row(GLA_V_W), row(GLA_K_W)],
        out_shape=[sds((B, FOX_W, S), F32), sds((B, FOX_W, S), F32), sds((B, FOX_W, S), BF16),
                   sds((B, FOX_W, S), BF16), sds((B, S, FOX_W), BF16), sds((B, S, LANES), BF16),
                   sds((B, FOX_HEADS, S), F32),
                   sds((B, S, GLA_K_W), F32), sds((B, S, GLA_K_W), F32), sds((B, S, GLA_V_W), BF16),
                   sds((B, S, GLA_K_W), F32)],
        scratch_shapes=[pltpu.VMEM((1, LANES), F32)],
        compiler_params=_params(2),
        name="in_proj",
    )(x, g, wqkv, wgla, wsm, wau, bal, bfr, tri)


def _fox_body(k_ref, fa_ref, qT_ref, vT_ref, o_ref, rhs_ref, acc_ref, m_ref, *, t):
    i = pl.program_id(1)
    npair = FOX_HEADS // 2
    ONES = 16

    @pl.when((pl.program_id(0) == 0) & (i == 0))
    def _():
        rr = lax.broadcasted_iota(jnp.int32, (LANES, 2 * t), 0)
        cc = lax.broadcasted_iota(jnp.int32, (LANES, 2 * t), 1)
        head = rr & (FOX_HEADS - 1)
        for p in range(npair):
            mine = head == jnp.where(cc < t, 2 * p, 2 * p + 1)
            neg = jnp.where(rr < F_PARTS * FOX_HEADS, jnp.where(mine, -1.0, 0.0), 0.0)
            rhs_ref[p, LANES:, :] = neg.astype(BF16)
            rhs_ref[p, :FOX_HD, t:] = jnp.zeros((FOX_HD, t), BF16)
            rhs_ref[p, FOX_HD:LANES, :t] = jnp.zeros((FOX_HD, t), BF16)

    for p in range(npair):
        rhs_ref[p, :FOX_HD, :t] = qT_ref[p * LANES:p * LANES + FOX_HD, :]
        rhs_ref[p, FOX_HD:LANES, t:] = qT_ref[p * LANES + FOX_HD:(p + 1) * LANES, :]
    m_ref[...] = jnp.full(m_ref.shape, -jnp.inf, F32)
    acc_ref[...] = jnp.zeros(acc_ref.shape, F32)
    ones = jnp.where(lax.broadcasted_iota(jnp.int32, (ONES, t), 0) == 0, 1.0, 0.0).astype(BF16)

    def run(blocks):
        work = []
        for j, diagonal in blocks:
            c0 = pl.multiple_of(j * t, t)
            fa = fa_ref[pl.ds(c0, t), :]
            work += [(c0, fa, diagonal, p) for p in range(npair)]

        def scores(c0, fa, diagonal, p):
            lhs = jnp.concatenate([k_ref[pl.ds(c0, t), p * LANES:(p + 1) * LANES], fa], axis=1)
            sT = _dot(lhs, rhs_ref[p])
            if diagonal:
                key = lax.broadcasted_iota(jnp.int32, (t, 2 * t), 0)
                qry = lax.broadcasted_iota(jnp.int32, (t, 2 * t), 1)
                sT = jnp.where(key <= jnp.where(qry >= t, qry - t, qry), sT, -jnp.inf)
            return sT

        def soft(p, sT):
            m_old = m_ref[p]
            m_new = jnp.maximum(m_old, jnp.max(sT, axis=0, keepdims=True))
            m_ref[p] = m_new
            return jnp.exp2(m_old - m_new), jnp.exp2(sT - m_new).astype(BF16)

        def pv(c0, p, alpha, pT):
            for e in range(2):
                r0 = p * LANES + e * FOX_HD
                v1 = jnp.concatenate([vT_ref[r0:r0 + FOX_HD, pl.ds(c0, t)], ones], axis=0)
                cols = slice(e * t, (e + 1) * t)
                acc_ref[p, e] = alpha[:, cols] * acc_ref[p, e] + _dot(v1, pT[:, cols])

        n = len(work)
        sT = {x: scores(*work[x]) for x in range(2)}
        done = {}
        for x in range(n):
            done[x] = soft(work[x][3], sT.pop(x))
            if x + 2 < n:
                sT[x + 2] = scores(*work[x + 2])
            if x >= 1:
                pv(work[x - 1][0], work[x - 1][3], *done.pop(x - 1))
        pv(work[n - 1][0], work[n - 1][3], *done.pop(n - 1))

    def two_blocks(jj, carry):
        run([(2 * jj, False), (2 * jj + 1, False)])
        return carry

    lax.fori_loop(0, lax.shift_right_logical(i, 1), two_blocks, 0)

    @pl.when((i & 1) == 1)
    def _():
        run([(i - 1, False), (i, True)])

    @pl.when((i & 1) == 0)
    def _():
        run([(i, True)])

    for p in range(npair):
        halves = []
        for e in range(2):
            acc = acc_ref[p, e]
            halves.append(acc[:FOX_HD] * (1.0 / acc[FOX_HD:FOX_HD + 1]))
        o_ref[:, p * LANES:(p + 1) * LANES] = jnp.concatenate(halves, axis=0).T.astype(o_ref.dtype)


def _fox_prompt(kb, fa, qTb, vTb, t):
    B, S, _ = kb.shape
    npair = FOX_HEADS // 2
    return pl.pallas_call(
        functools.partial(_fox_body, t=t),
        grid=(B, S // t),
        in_specs=[pl.BlockSpec((None, S, FOX_W), lambda b, i: (b, 0, 0)),
                  pl.BlockSpec((None, S, LANES), lambda b, i: (b, 0, 0)),
                  pl.BlockSpec((None, FOX_W, t), lambda b, i: (b, 0, i)),
                  pl.BlockSpec((None, FOX_W, S), lambda b, i: (b, 0, 0))],
        out_specs=pl.BlockSpec((None, t, FOX_W), lambda b, i: (b, i, 0)),
        out_shape=jax.ShapeDtypeStruct((B, S, FOX_W), BF16),
        scratch_shapes=[pltpu.VMEM((npair, 2 * LANES, 2 * t), BF16), pltpu.VMEM((npair, 2, FOX_HD + 16, t), F32),
                        pltpu.VMEM((npair, 1, 2 * t), F32)],
        compiler_params=_params(2),
        name="fox_prompt",
    )(kb, fa, qTb, vTb)


def _gla_body(qg_ref, kg_ref, vg_ref, la_ref, s0_ref, o_ref, sN_ref, st_ref, *, bt, C):
    c = pl.program_id(1)
    KW, VW = GLA_K_W, GLA_V_W
    blk = (lax.broadcasted_iota(jnp.int32, (VW, KW), 0) // GLA_DV
           == lax.broadcasted_iota(jnp.int32, (VW, KW), 1) // GLA_DK)

    @pl.when(c == 0)
    def _():
        for b in range(bt):
            s0 = s0_ref[b]
            rows = [jnp.concatenate([s0[h] if hh == h else jnp.zeros((GLA_DK, GLA_DV), F32)
                                     for hh in range(GLA_HEADS)], axis=1) for h in range(GLA_HEADS)]
            st_ref[b] = jnp.concatenate(rows, axis=0).T

    tri = (lax.broadcasted_iota(jnp.int32, (C, C), 1) <= lax.broadcasted_iota(jnp.int32, (C, C), 0))
    tri_b = tri.astype(BF16)
    tri4 = jnp.concatenate([tri] * GLA_HEADS, axis=0)
    head_of_lane = lax.broadcasted_iota(jnp.int32, (C, KW), 1) // GLA_DK

    def chain(b):
        bc = sum(_dot(tri_b, part) for part in _split3(la_ref[b]))
        yield
        ref = bc[C // 2 - 1:C // 2, :]
        last = bc[C - 1:C, :]
        q = qg_ref[b]
        k = kg_ref[b]
        v = vg_ref[b]
        st = st_ref[b]
        qi = (q * jnp.exp(bc)).astype(BF16)
        qt = q * jnp.exp(bc - ref)
        kt = (k * jnp.exp(ref - bc)).astype(BF16)
        kd = (k * jnp.exp(last - bc)).astype(BF16)
        qm = jnp.concatenate([jnp.where(head_of_lane == h, qt, 0.0) for h in range(GLA_HEADS)],
                             axis=0).astype(BF16)
        a_raw = _dot_nt(qm, kt)
        inter = _dot_nt(qi, st.astype(BF16))
        upd = _dot(v.astype(F32).T.astype(BF16), kd)
        yield
        A = jnp.where(tri4, a_raw, 0.0).astype(BF16)
        intra = jnp.concatenate([_dot(A[h * C:(h + 1) * C], v[:, h * GLA_DV:(h + 1) * GLA_DV])
                                 for h in range(GLA_HEADS)], axis=1)
        yield
        o_ref[b] = inter + intra
        st_ref[b] = jnp.where(blk, st * jnp.exp(last) + upd, 0.0)
        yield

    width = math.gcd(bt, GLA_INTERLEAVE)

    def group(i, _):
        chains = [chain(i * width + r) for r in range(width)]
        for _stage in range(4):
            for ch in chains:
                next(ch)
        return 0

    lax.fori_loop(0, bt // width, group, 0)

    @pl.when(c == pl.num_programs(1) - 1)
    def _():
        for b in range(bt):
            s = st_ref[b].T
            for h in range(GLA_HEADS):
                sN_ref[b, h] = s[h * GLA_DK:(h + 1) * GLA_DK, h * GLA_DV:(h + 1) * GLA_DV]


def _gla(qg, kg, vg, la, s0, bt):
    B, S, _ = qg.shape
    C = GLA_CHUNK
    spec = lambda w: pl.BlockSpec((bt, C, w), lambda g, c: (g, c, 0))
    sspec = pl.BlockSpec((bt, GLA_HEADS, GLA_DK, GLA_DV), lambda g, c: (g, 0, 0, 0))
    return pl.pallas_call(
        functools.partial(_gla_body, bt=bt, C=C),
        grid=(B // bt, S // C),
        in_specs=[spec(GLA_K_W), spec(GLA_K_W), spec(GLA_V_W), spec(GLA_K_W), sspec],
        out_specs=[spec(GLA_V_W), sspec],
        out_shape=[jax.ShapeDtypeStruct((B, S, GLA_V_W), F32),
                   jax.ShapeDtypeStruct((B, GLA_HEADS, GLA_DK, GLA_DV), F32)],
        scratch_shapes=[pltpu.VMEM((bt, GLA_V_W, GLA_K_W), F32)],
        compiler_params=_params(2),
        name="gla_scan",
    )(qg, kg, vg, la, s0)


DECODE_SUBSTEPS = 8
MLP_CHUNKS = 4


def _finish_stages(x_ref, of_ref, og_ref, g1_ref, gn_ref, g2_ref, g3_ref, g4_ref, wog_ref, wgf_ref, wgg_ref, wfo_ref,
                   wgo_ref, wo_ref, wu_ref, wd_ref, y_ref):
    x = x_ref[...]
    h = _rms(x, g1_ref[...]).astype(BF16)
    gate = _dot(h, wog_ref[...])
    gf = _dot(h, wgf_ref[...])
    yield
    gg = _dot(h, wgg_ref[...])
    fo = _dot(of_ref[...], wfo_ref[...])
    o = og_ref[...]
    ys = []
    for hh in range(GLA_HEADS):
        sl = slice(hh * GLA_DV, (hh + 1) * GLA_DV)
        gt = gate[:, sl]
        ys.append(_rms(o[:, sl], gn_ref[...]) * (gt * _sigmoid(gt)))
    y_gla = jnp.concatenate(ys, axis=1).astype(BF16)
    yield
    u = _sigmoid(gf) * fo + _sigmoid(gg) * _dot(y_gla, wgo_ref[...])
    x1 = x + _rms(_dot(u.astype(BF16), wo_ref[...]), g2_ref[...])
    h2 = _rms(x1, g3_ref[...]).astype(BF16)
    yield
    ff = wu_ref.shape[1] // MLP_CHUNKS
    acc = jnp.zeros(x.shape, F32)
    for c in range(MLP_CHUNKS):
        up = jnp.maximum(_dot(h2, wu_ref[:, c * ff:(c + 1) * ff]), 0.0)
        if c == MLP_CHUNKS - 1:
            yield
        acc = acc + _dot((up * up).astype(BF16), wd_ref[c * ff:(c + 1) * ff, :])
        if c < MLP_CHUNKS - 1:
            yield
    y_ref[...] = x1 + _rms(acc, g4_ref[...])
    yield


def _finish_body(*refs):
    for _ in _finish_stages(*refs):
        pass


def _finish_decode_body(pt_ref, *refs, G, NP, RING):
    fin, dec = refs[:16], refs[16:25]
    y_ref, o_ref = refs[25:27]
    scratch = refs[27:]
    i = pl.program_id(0)
    n_sub = pl.num_programs(0) * DECODE_SUBSTEPS
    stages = _finish_stages(*fin, y_ref)
    for r in range(DECODE_SUBSTEPS):
        next(stages)
        _decode_step(pt_ref, dec, o_ref, scratch, i * DECODE_SUBSTEPS + r, n_sub, r, DECODE_SUBSTEPS, G, NP, RING)


def _finish(x, o_fox, o_gla, W, tm):
    N, D = x.shape
    row = lambda w: pl.BlockSpec((tm, w), lambda i: (i, 0))
    full = lambda a: pl.BlockSpec(a.shape, lambda i: (0,) * a.ndim, pipeline_mode=pl.Buffered(1))
    consts = [W[n] for n in ("g1", "gn", "g2", "g3", "g4", "wog", "wgf", "wgg", "wfo", "wgo", "wo", "wu", "wd")]
    return pl.pallas_call(
        _finish_body,
        grid=(N // tm,),
        in_specs=[row(D), row(FOX_W), row(GLA_V_W)] + [full(a) for a in consts],
        out_specs=row(D),
        out_shape=jax.ShapeDtypeStruct((N, D), F32),
        compiler_params=_params(1),
        name="finish",
    )(x, o_fox, o_gla, *consts)


def _decode_step(pt_ref, dec, o_ref, scratch, step, n_steps, r, period, G, NP, RING):
    kc_hbm, vc_hbm, fc_hbm, q_ref, kn_ref, vn_ref, cn_ref, qrow_ref, w_ref = dec
    kbuf, vbuf, fbuf, sem, qs_ref, acc_ref, car_ref, m_ref, l_ref, qbd_ref = scratch
    ng = NP // G
    g = lax.rem(step, jnp.int32(ng))
    _, _, H, HD, PAGE = kbuf.shape
    reachable = lambda gval: (gval - r) % math.gcd(ng, period) == 0

    def page_copies(s, slot):
        bb, gg = (s // ng, s % ng) if isinstance(s, int) else (lax.div(s, jnp.int32(ng)), lax.rem(s, jnp.int32(ng)))
        out = []
        for j in range(G):
            page = pt_ref[bb, NP - 1 - (gg * G + j)]
            out.append(pltpu.make_async_copy(kc_hbm.at[0, page], kbuf.at[slot, j], sem.at[0, slot]))
            out.append(pltpu.make_async_copy(vc_hbm.at[0, page], vbuf.at[slot, j], sem.at[1, slot]))
            out.append(pltpu.make_async_copy(fc_hbm.at[0, page], fbuf.at[slot, j], sem.at[2, slot]))
        return out

    if r == 0:
        @pl.when(step == 0)
        def _():
            for s in range(RING - 1):
                for cp in page_copies(s, s):
                    cp.start()

    ahead = step + (RING - 1)
    for cp in page_copies(jnp.minimum(ahead, n_steps - 1), lax.rem(ahead, jnp.int32(RING))):
        cp.start()

    slot = lax.rem(step, jnp.int32(RING))
    for cp in page_copies(step, slot):
        cp.wait()
    k_refs = [kbuf.at[slot, j] for j in range(G)]
    v_refs = [vbuf.at[slot, j] for j in range(G)]
    f_refs = [fbuf.at[slot, j] for j in range(G)]

    if reachable(0):
        @pl.when(g == 0)
        def _():
            qs_ref[...] = jnp.broadcast_to(q_ref[...], (H, HD, PAGE))
            wide = jnp.concatenate([qrow_ref[...]] * H, axis=1)
            mine = (lax.broadcasted_iota(jnp.int32, wide.shape, 1) // HD
                    == lax.broadcasted_iota(jnp.int32, wide.shape, 0))
            qbd_ref[...] = jnp.concatenate([jnp.where(mine, wide, 0.0), jnp.zeros_like(wide)], axis=0).astype(BF16)
            car_ref[...] = jnp.broadcast_to(cn_ref[...], (H, PAGE))
            m_ref[...] = jnp.full(m_ref.shape, -jnp.inf, F32)
            l_ref[...] = jnp.zeros(l_ref.shape, F32)
            acc_ref[...] = jnp.zeros(acc_ref.shape, F32)

    lf = jnp.concatenate([f_refs[j][...] for j in range(G)], axis=0)
    w = w_ref[...]
    suf = sum(_dot(part, w) for part in _split3(lf))
    carry = car_ref[...]
    bias = []
    for j in range(G):
        bias.append(suf[j * H:(j + 1) * H, :PAGE] + carry)
        carry = carry + suf[j * H:(j + 1) * H, PAGE:]
    car_ref[...] = carry

    qbd = qbd_ref[...]
    logits = [_dot(qbd, k_refs[j][...].reshape(H * HD, PAGE).astype(BF16))[:H] + bias[j] for j in range(G)]

    for h in range(H):
        rows = [logits[j][h:h + 1, :] for j in range(G)]
        m_old = m_ref[h]
        m_new = jnp.maximum(m_old, functools.reduce(jnp.maximum, rows))
        alpha = jnp.exp(m_old - m_new)
        a = acc_ref[h] * alpha
        ls = l_ref[h] * alpha
        for j in range(G):
            pr = jnp.exp(rows[j] - m_new)
            ls = ls + pr
            a = a + v_refs[j][h] * pr
        acc_ref[h] = a
        l_ref[h] = ls
        m_ref[h] = m_new

    if reachable(ng - 1):
        @pl.when(g == ng - 1)
        def _():
            for h in range(H):
                ln = jnp.sum(qs_ref[h] * kn_ref[h], axis=0, keepdims=True)
                m = m_ref[h]
                mf = jnp.maximum(jnp.max(m, axis=1, keepdims=True), ln)
                wl = jnp.exp(m - mf)
                pn = jnp.exp(ln - mf)[:, 0:1]
                den = jnp.sum(l_ref[h] * wl, axis=1, keepdims=True) + pn
                num = jnp.sum(acc_ref[h] * wl, axis=1, keepdims=True) + pn * vn_ref[h]
                o_ref[h] = num / den

    if r >= period - (RING - 1):
        @pl.when(step >= n_steps - (RING - 1))
        def _():
            for cp in page_copies(n_steps - 1, lax.rem(ahead, jnp.int32(RING))):
                cp.wait()


def _finish_decode(x, o_fox, o_gla, W, tm, kc, vc, fc, page_table, q, kn, vn, cn, qrow, G):
    N, D = x.shape
    DB, NP = page_table.shape
    _, _, H, HD, PAGE = kc.shape
    ng = NP // G
    n_tiles = N // tm
    assert ng % DECODE_SUBSTEPS == 0 and n_tiles * DECODE_SUBSTEPS == DB * ng
    tiles_per_seq = ng // DECODE_SUBSTEPS
    pos = lax.broadcasted_iota(jnp.int32, (PAGE, 2 * PAGE), 0)
    lane = lax.broadcasted_iota(jnp.int32, (PAGE, 2 * PAGE), 1)
    wsuf = jnp.where(lane < PAGE, pos > lane, True).astype(BF16)

    row = lambda w: pl.BlockSpec((tm, w), lambda i, pt: (i, 0))
    full = lambda a: pl.BlockSpec(a.shape, lambda i, pt: (0,) * a.ndim, pipeline_mode=pl.Buffered(1))
    consts = [W[n] for n in ("g1", "gn", "g2", "g3", "g4", "wog", "wgf", "wgg", "wfo", "wgo", "wo", "wu", "wd")]
    hbm = pl.BlockSpec(memory_space=pl.ANY)
    col = pl.BlockSpec((None, H, HD, 1), lambda i, pt: (i // tiles_per_seq, 0, 0, 0))
    specs = ([row(D), row(FOX_W), row(GLA_V_W)] + [full(a) for a in consts]
             + [hbm, hbm, hbm, col, col, col, pl.BlockSpec((None, H, 1), lambda i, pt: (i // tiles_per_seq, 0, 0)),
                pl.BlockSpec((None, H, HD), lambda i, pt: (i // tiles_per_seq, 0, 0)), full(wsuf)])
    return pl.pallas_call(
        functools.partial(_finish_decode_body, G=G, NP=NP, RING=DECODE_RING),
        grid_spec=pltpu.PrefetchScalarGridSpec(
            num_scalar_prefetch=1,
            grid=(n_tiles,),
            in_specs=specs,
            out_specs=[row(D), col],
            scratch_shapes=[pltpu.VMEM((DECODE_RING, G, H, HD, PAGE), F32),
                            pltpu.VMEM((DECODE_RING, G, H, HD, PAGE), F32),
                            pltpu.VMEM((DECODE_RING, G, H, PAGE), F32),
                            pltpu.SemaphoreType.DMA((3, DECODE_RING)),
                            pltpu.VMEM((H, HD, PAGE), F32), pltpu.VMEM((H, HD, PAGE), F32),
                            pltpu.VMEM((H, PAGE), F32), pltpu.VMEM((H, 1, PAGE), F32),
                            pltpu.VMEM((H, 1, PAGE), F32), pltpu.VMEM((2 * H, H * HD), BF16)]),
        out_shape=[jax.ShapeDtypeStruct((N, D), F32), jax.ShapeDtypeStruct((DB, H, HD, 1), F32)],
        compiler_params=_params(1),
        name="finish_decode",
    )(page_table, x, o_fox, o_gla, *consts, kc, vc, fc, q, kn, vn, cn, qrow, wsuf)


def _layer_weights(l, g_pre_mix, w_in, b_f, w_alpha_up, b_alpha, g_gla_norm, w_fox_out, w_gla_out, w_o,
                   g_post_mix, g_pre_mlp, w_up, w_down, g_post_mlp):
    sizes = (FOX_W, FOX_W, FOX_W, FOX_HEADS, GLA_K_W, GLA_K_W, GLA_V_W, GLA_V_W, GLA_RANK)
    off = [0]
    for s in sizes:
        off.append(off[-1] + s)
    w = w_in[l]
    D = w.shape[0]
    wb = w.astype(BF16)
    wqkv = wb[:, :off[3]]
    wgla = wb[:, off[4]:off[7]]
    wog = wb[:, off[7]:off[8]]
    wgf = wb[:, off[9]:off[9] + D]
    wgg = wb[:, off[9] + D:off[9] + 2 * D]
    nf = F_PARTS * FOX_HEADS
    pad = LANES - nf - GLA_RANK
    wsm = jnp.concatenate([wb[:, off[3]:off[4]]] * F_PARTS + [wb[:, off[8]:off[9]], jnp.zeros((D, pad), BF16)], axis=1)
    wau = jnp.concatenate([jnp.zeros((nf, GLA_K_W), BF16), w_alpha_up[l].astype(BF16),
                           jnp.zeros((pad, GLA_K_W), BF16)], axis=0)
    bfr = jnp.concatenate([b_f[l]] * F_PARTS + [jnp.zeros((LANES - nf,), F32)])[None, :]
    r = lambda a: a[l][None, :]
    return dict(g1=r(g_pre_mix), wqkv=wqkv, wgla=wgla, wsm=wsm, wau=wau, bal=r(b_alpha), bfr=bfr,
                gn=r(g_gla_norm), wog=wog, wgf=wgf, wgg=wgg, wfo=w_fox_out[l].astype(BF16),
                wgo=w_gla_out[l].astype(BF16), wo=w_o[l].astype(BF16), g2=r(g_post_mix), g3=r(g_pre_mlp),
                wu=w_up[l].astype(BF16), wd=w_down[l].astype(BF16), g4=r(g_post_mlp))


def kernel(x_prompt, x_sample, cache_k, cache_v, cache_logf, state_gla, page_table, g_pre_mix, w_in, b_f, w_alpha_up, b_alpha, g_gla_norm, w_fox_out, w_gla_out, w_o, g_post_mix, g_pre_mlp, w_up, w_down, g_post_mlp):
    B, S, D = x_prompt.shape
    DB = x_sample.shape[0]
    depth = w_in.shape[0]
    assert depth == 1 and x_sample.shape[1] == 1
    W = _layer_weights(0, g_pre_mix, w_in, b_f, w_alpha_up, b_alpha, g_gla_norm, w_fox_out, w_gla_out, w_o,
                       g_post_mix, g_pre_mlp, w_up, w_down, g_post_mlp)
    proj = lambda x, tm, sc: _in_proj(x, W["g1"], W["wqkv"], W["wgla"], W["wsm"], W["wau"], W["bal"], W["bfr"], tm, sc)

    PADT = LANES
    xs = jnp.zeros((1, PADT, D), F32).at[0, :DB].set(x_sample[:, 0])
    kT, vT, _, qTb, _, _, lfT, qg_s, kg_s, vg_s, la_s = proj(xs, PADT, 1.0)
    k_s = kT[0].T[:DB].reshape(DB, FOX_HEADS, FOX_HD)
    v_s = vT[0].T[:DB].reshape(DB, FOX_HEADS, FOX_HD)
    lf_s = lfT[0].T[:DB]
    q_s = qTb[0].T[:DB].astype(F32).reshape(DB, FOX_HEADS, FOX_HD)
    kc = jnp.transpose(cache_k, (0, 1, 3, 4, 2))
    vc = jnp.transpose(cache_v, (0, 1, 3, 4, 2))
    fc = jnp.transpose(cache_logf, (0, 1, 3, 2))

    kT, vT, vTb, qTb, kb, fa, lfT, qg, kg, vg, la = proj(x_prompt, ROW_TILE, LOG2E)
    o_fox = _fox_prompt(kb, fa, qTb, vTb, ATT_TILE)
    o_gla, s_p = _gla(qg, kg, vg, la, jnp.zeros((B, GLA_HEADS, GLA_DK, GLA_DV), F32), B)
    y_p, o_fs = _finish_decode(x_prompt.reshape(B * S, D), o_fox.reshape(B * S, FOX_W), o_gla.reshape(B * S, GLA_V_W),
                               W, ROW_TILE, kc, vc, fc, page_table, q_s[..., None], k_s[..., None], v_s[..., None],
                               lf_s[..., None], q_s, PAGES_PER_STEP)
    y_p = y_p.reshape(B, S, D)
    to_tok = lambda a: jnp.transpose(a.reshape(1, B, FOX_HEADS, FOX_HD, S), (0, 1, 4, 2, 3))
    k_p, v_p = to_tok(kT), to_tok(vT)
    lf_p = jnp.transpose(lfT, (0, 2, 1))[None]

    o_fox_s = jnp.zeros((PADT, FOX_W), BF16).at[:DB].set(o_fs.reshape(DB, FOX_W).astype(BF16))
    C = GLA_CHUNK
    pad_c = lambda a: jnp.zeros((DB, C, a.shape[-1]), a.dtype).at[:, 0].set(a[0, :DB])
    o_gs, s_s = _gla(pad_c(qg_s), pad_c(kg_s), pad_c(vg_s), pad_c(la_s), state_gla[0], DB // 2)
    o_gla_s = jnp.zeros((PADT, GLA_V_W), F32).at[:DB].set(o_gs[:, 0])
    y_s = _finish(xs[0], o_fox_s, o_gla_s, W, PADT)[:DB].reshape(DB, 1, D)

    return (y_p, y_s, k_p, v_p, lf_p, s_p[None],
            k_s[None, :, None], v_s[None, :, None], lf_s[None, :, None], s_s[None])
```

```python
import functools
import math

import jax
import jax.numpy as jnp
from jax import lax
from jax.experimental import pallas as pl
from jax.experimental.pallas import tpu as pltpu

F32 = jnp.float32
BF16 = jnp.bfloat16

FOX_HEADS = 8
FOX_HD = 64
FOX_W = FOX_HEADS * FOX_HD
GLA_HEADS = 4
GLA_DK = 64
GLA_DV = 128
GLA_K_W = GLA_HEADS * GLA_DK
GLA_V_W = GLA_HEADS * GLA_DV
GLA_RANK = 16
GLA_GATE_NORM = 16.0
GLA_CHUNK = 64
EPS = 1e-6
LANES = 128
VMEM_LIMIT = 56 * 1024 * 1024
LOG2E = 1.4426950408889634
F_PARTS = 3

ROW_TILE = 512
ATT_TILE = 256
PAGES_PER_STEP = 8
DECODE_RING = 2
CUM_BLOCK = 128
GLA_INTERLEAVE = 8


def _params(n_axes, flags=None):
    return pltpu.CompilerParams(dimension_semantics=("arbitrary",) * n_axes,
                                vmem_limit_bytes=VMEM_LIMIT, flags=flags)


def _log_sigmoid(z):
    return jnp.minimum(z, 0.0) - jnp.log1p(jnp.exp(-jnp.abs(z)))


def _sigmoid(z):
    return 1.0 / (1.0 + jnp.exp(-z))


def _rms(x, g):
    return x * lax.rsqrt(jnp.mean(x * x, axis=-1, keepdims=True) + EPS) * g


def _dot(a, b):
    return jnp.dot(a, b, preferred_element_type=F32)


def _dot_nt(a, b):
    return lax.dot_general(a, b, (((1,), (1,)), ((), ())), preferred_element_type=F32)


def _split3(x):
    hi = x.astype(BF16)
    r1 = x - hi.astype(F32)
    mid = r1.astype(BF16)
    lo = (r1 - mid.astype(F32)).astype(BF16)
    return hi, mid, lo


def _in_proj_body(x_ref, g_ref, wqkv_ref, wgla_ref, wsm_ref, wau_ref, bal_ref, bf_ref, tri_ref,
                  kT_ref, vT_ref, vTb_ref, qTb_ref, kb_ref, fa_ref, lfT_ref, qg_ref, kg_ref, vg_ref, la_ref,
                  car_ref, *, logit_scale):
    @pl.when(pl.program_id(1) == 0)
    def _():
        car_ref[...] = jnp.zeros_like(car_ref)

    h = _rms(x_ref[...], g_ref[...]).astype(BF16)
    sm = _dot(h, wsm_ref[...])
    qkv = _dot(h, wqkv_ref[...])
    lf = _log_sigmoid(sm + bf_ref[...])
    lfT_ref[...] = lf.T[:FOX_HEADS, :]
    z = _dot(sm.astype(BF16), wau_ref[...]) + bal_ref[...]
    la_ref[...] = _log_sigmoid(z) * (1.0 / GLA_GATE_NORM)
    tri = tri_ref[...]
    parts = _split3(lf)
    run = car_ref[...]
    blocks = []
    for r0 in range(0, lf.shape[0], CUM_BLOCK):
        blocks.append(run + sum(_dot(tri, part[r0:r0 + CUM_BLOCK]) for part in parts))
        run = blocks[-1][CUM_BLOCK - 1:, :]
    F = jnp.concatenate(blocks, axis=0)
    car_ref[...] = run
    hi, mid, lo = _split3(F * logit_scale)
    lane = lax.broadcasted_iota(jnp.int32, F.shape, 1)
    zero = jnp.zeros_like(hi)
    fa_ref[...] = jnp.where(lane < FOX_HEADS, hi,
                            jnp.where(lane < 2 * FOX_HEADS, mid, jnp.where(lane < 3 * FOX_HEADS, lo, zero)))
    gl = _dot(h, wgla_ref[...])
    q = qkv[:, :FOX_W] * (FOX_HD ** -0.5 * logit_scale)
    k = qkv[:, FOX_W:2 * FOX_W]
    v = qkv[:, 2 * FOX_W:]
    qTb_ref[...] = q.T.astype(BF16)
    kb_ref[...] = k.astype(BF16)
    kT_ref[...] = k.T
    vT = v.T
    vT_ref[...] = vT
    vTb_ref[...] = vT.astype(BF16)
    qg_ref[...] = gl[:, :GLA_K_W] * (GLA_DK ** -0.5)
    kg_ref[...] = gl[:, GLA_K_W:2 * GLA_K_W]
    vg_ref[...] = gl[:, 2 * GLA_K_W:].astype(BF16)


def _in_proj(x, g, wqkv, wgla, wsm, wau, bal, bfr, tm, logit_scale):
    B, S, D = x.shape
    nt = S // tm
    assert tm % CUM_BLOCK == 0
    tri = (lax.broadcasted_iota(jnp.int32, (CUM_BLOCK, CUM_BLOCK), 1)
           <= lax.broadcasted_iota(jnp.int32, (CUM_BLOCK, CUM_BLOCK), 0)).astype(BF16)
    row = lambda w: pl.BlockSpec((None, tm, w), lambda b, i: (b, i, 0))
    col = lambda w: pl.BlockSpec((None, w, tm), lambda b, i: (b, 0, i))
    full = lambda a: pl.BlockSpec(a.shape, lambda b, i: (0,) * a.ndim)
    sds = jax.ShapeDtypeStruct
    return pl.pallas_call(
        functools.partial(_in_proj_body, logit_scale=logit_scale),
        grid=(B, nt),
        in_specs=[row(D), full(g), full(wqkv), full(wgla), full(wsm), full(wau), full(bal), full(bfr), full(tri)],
        out_specs=[col(FOX_W), col(FOX_W), col(FOX_W), col(FOX_W), row(FOX_W), row(LANES), col(FOX_HEADS),
                   row(GLA_K_W), row(GLA_K_W), row(GLA_V_W), row(GLA_K_W)],
        out_shape=[sds((B, FOX_W, S), F32), sds((B, FOX_W, S), F32), sds((B, FOX_W, S), BF16),
                   sds((B, FOX_W, S), BF16), sds((B, S, FOX_W), BF16), sds((B, S, LANES), BF16),
                   sds((B, FOX_HEADS, S), F32),
                   sds((B, S, GLA_K_W), F32), sds((B, S, GLA_K_W), F32), sds((B, S, GLA_V_W), BF16),
                   sds((B, S, GLA_K_W), F32)],
        scratch_shapes=[pltpu.VMEM((1, LANES), F32)],
        compiler_params=_params(2),
        name="in_proj",
    )(x, g, wqkv, wgla, wsm, wau, bal, bfr, tri)


def _fox_body(k_ref, fa_ref, qT_ref, vT_ref, o_ref, rhs_ref, acc_ref, m_ref, *, t):
    i = pl.program_id(1)
    npair = FOX_HEADS // 2
    ONES = 16

    @pl.when((pl.program_id(0) == 0) & (i == 0))
    def _():
        rr = lax.broadcasted_iota(jnp.int32, (LANES, 2 * t), 0)
        cc = lax.broadcasted_iota(jnp.int32, (LANES, 2 * t), 1)
        head = rr & (FOX_HEADS - 1)
        for p in range(npair):
            mine = head == jnp.where(cc < t, 2 * p, 2 * p + 1)
            neg = jnp.where(rr < F_PARTS * FOX_HEADS, jnp.where(mine, -1.0, 0.0), 0.0)
            rhs_ref[p, LANES:, :] = neg.astype(BF16)
            rhs_ref[p, :FOX_HD, t:] = jnp.zeros((FOX_HD, t), BF16)
            rhs_ref[p, FOX_HD:LANES, :t] = jnp.zeros((FOX_HD, t), BF16)

    for p in range(npair):
        rhs_ref[p, :FOX_HD, :t] = qT_ref[p * LANES:p * LANES + FOX_HD, :]
        rhs_ref[p, FOX_HD:LANES, t:] = qT_ref[p * LANES + FOX_HD:(p + 1) * LANES, :]
    m_ref[...] = jnp.full(m_ref.shape, -jnp.inf, F32)
    acc_ref[...] = jnp.zeros(acc_ref.shape, F32)
    ones = jnp.where(lax.broadcasted_iota(jnp.int32, (ONES, t), 0) == 0, 1.0, 0.0).astype(BF16)

    def run(blocks):
        work = []
        for j, diagonal in blocks:
            c0 = pl.multiple_of(j * t, t)
            fa = fa_ref[pl.ds(c0, t), :]
            work += [(c0, fa, diagonal, p) for p in range(npair)]

        def scores(c0, fa, diagonal, p):
            lhs = jnp.concatenate([k_ref[pl.ds(c0, t), p * LANES:(p + 1) * LANES], fa], axis=1)
            sT = _dot(lhs, rhs_ref[p])
            if diagonal:
                key = lax.broadcasted_iota(jnp.int32, (t, 2 * t), 0)
                qry = lax.broadcasted_iota(jnp.int32, (t, 2 * t), 1)
                sT = jnp.where(key <= jnp.where(qry >= t, qry - t, qry), sT, -jnp.inf)
            return sT

        def soft(p, sT):
            m_old = m_ref[p]
            m_new = jnp.maximum(m_old, jnp.max(sT, axis=0, keepdims=True))
            m_ref[p] = m_new
            return jnp.exp2(m_old - m_new), jnp.exp2(sT - m_new).astype(BF16)

        def pv(c0, p, alpha, pT):
            for e in range(2):
                r0 = p * LANES + e * FOX_HD
                v1 = jnp.concatenate([vT_ref[r0:r0 + FOX_HD, pl.ds(c0, t)], ones], axis=0)
                cols = slice(e * t, (e + 1) * t)
                acc_ref[p, e] = alpha[:, cols] * acc_ref[p, e] + _dot(v1, pT[:, cols])

        n = len(work)
        sT = {x: scores(*work[x]) for x in range(2)}
        done = {}
        for x in range(n):
            done[x] = soft(work[x][3], sT.pop(x))
            if x + 2 < n:
                sT[x + 2] = scores(*work[x + 2])
            if x >= 1:
                pv(work[x - 1][0], work[x - 1][3], *done.pop(x - 1))
        pv(work[n - 1][0], work[n - 1][3], *done.pop(n - 1))

    def two_blocks(jj, carry):
        run([(2 * jj, False), (2 * jj + 1, False)])
        return carry

    lax.fori_loop(0, lax.shift_right_logical(i, 1), two_blocks, 0)

    @pl.when((i & 1) == 1)
    def _():
        run([(i - 1, False), (i, True)])

    @pl.when((i & 1) == 0)
    def _():
        run([(i, True)])

    for p in range(npair):
        halves = []
        for e in range(2):
            acc = acc_ref[p, e]
            halves.append(acc[:FOX_HD] * (1.0 / acc[FOX_HD:FOX_HD + 1]))
        o_ref[:, p * LANES:(p + 1) * LANES] = jnp.concatenate(halves, axis=0).T.astype(o_ref.dtype)


def _fox_prompt(kb, fa, qTb, vTb, t):
    B, S, _ = kb.shape
    npair = FOX_HEADS // 2
    return pl.pallas_call(
        functools.partial(_fox_body, t=t),
        grid=(B, S // t),
        in_specs=[pl.BlockSpec((None, S, FOX_W), lambda b, i: (b, 0, 0)),
                  pl.BlockSpec((None, S, LANES), lambda b, i: (b, 0, 0)),
                  pl.BlockSpec((None, FOX_W, t), lambda b, i: (b, 0, i)),
                  pl.BlockSpec((None, FOX_W, S), lambda b, i: (b, 0, 0))],
        out_specs=pl.BlockSpec((None, t, FOX_W), lambda b, i: (b, i, 0)),
        out_shape=jax.ShapeDtypeStruct((B, S, FOX_W), BF16),
        scratch_shapes=[pltpu.VMEM((npair, 2 * LANES, 2 * t), BF16), pltpu.VMEM((npair, 2, FOX_HD + 16, t), F32),
                        pltpu.VMEM((npair, 1, 2 * t), F32)],
        compiler_params=_params(2),
        name="fox_prompt",
    )(kb, fa, qTb, vTb)


def _gla_body(qg_ref, kg_ref, vg_ref, la_ref, s0_ref, o_ref, sN_ref, st_ref, *, bt, C):
    c = pl.program_id(1)
    KW, VW = GLA_K_W, GLA_V_W
    blk = (lax.broadcasted_iota(jnp.int32, (VW, KW), 0) // GLA_DV
           == lax.broadcasted_iota(jnp.int32, (VW, KW), 1) // GLA_DK)

    @pl.when(c == 0)
    def _():
        for b in range(bt):
            s0 = s0_ref[b]
            rows = [jnp.concatenate([s0[h] if hh == h else jnp.zeros((GLA_DK, GLA_DV), F32)
                                     for hh in range(GLA_HEADS)], axis=1) for h in range(GLA_HEADS)]
            st_ref[b] = jnp.concatenate(rows, axis=0).T

    tri = (lax.broadcasted_iota(jnp.int32, (C, C), 1) <= lax.broadcasted_iota(jnp.int32, (C, C), 0))
    tri_b = tri.astype(BF16)
    tri4 = jnp.concatenate([tri] * GLA_HEADS, axis=0)
    head_of_lane = lax.broadcasted_iota(jnp.int32, (C, KW), 1) // GLA_DK

    def chain(b):
        bc = sum(_dot(tri_b, part) for part in _split3(la_ref[b]))
        yield
        ref = bc[C // 2 - 1:C // 2, :]
        last = bc[C - 1:C, :]
        q = qg_ref[b]
        k = kg_ref[b]
        v = vg_ref[b]
        st = st_ref[b]
        qi = (q * jnp.exp(bc)).astype(BF16)
        qt = q * jnp.exp(bc - ref)
        kt = (k * jnp.exp(ref - bc)).astype(BF16)
        kd = (k * jnp.exp(last - bc)).astype(BF16)
        qm = jnp.concatenate([jnp.where(head_of_lane == h, qt, 0.0) for h in range(GLA_HEADS)],
                             axis=0).astype(BF16)
        a_raw = _dot_nt(qm, kt)
        inter = _dot_nt(qi, st.astype(BF16))
        upd = _dot(v.astype(F32).T.astype(BF16), kd)
        yield
        A = jnp.where(tri4, a_raw, 0.0).astype(BF16)
        intra = jnp.concatenate([_dot(A[h * C:(h + 1) * C], v[:, h * GLA_DV:(h + 1) * GLA_DV])
                                 for h in range(GLA_HEADS)], axis=1)
        yield
        o_ref[b] = inter + intra
        st_ref[b] = jnp.where(blk, st * jnp.exp(last) + upd, 0.0)
        yield

    width = math.gcd(bt, GLA_INTERLEAVE)

    def group(i, _):
        chains = [chain(i * width + r) for r in range(width)]
        for _stage in range(4):
            for ch in chains:
                next(ch)
        return 0

    lax.fori_loop(0, bt // width, group, 0)

    @pl.when(c == pl.num_programs(1) - 1)
    def _():
        for b in range(bt):
            s = st_ref[b].T
            for h in range(GLA_HEADS):
                sN_ref[b, h] = s[h * GLA_DK:(h + 1) * GLA_DK, h * GLA_DV:(h + 1) * GLA_DV]


def _gla(qg, kg, vg, la, s0, bt):
    B, S, _ = qg.shape
    C = GLA_CHUNK
    spec = lambda w: pl.BlockSpec((bt, C, w), lambda g, c: (g, c, 0))
    sspec = pl.BlockSpec((bt, GLA_HEADS, GLA_DK, GLA_DV), lambda g, c: (g, 0, 0, 0))
    return pl.pallas_call(
        functools.partial(_gla_body, bt=bt, C=C),
        grid=(B // bt, S // C),
        in_specs=[spec(GLA_K_W), spec(GLA_K_W), spec(GLA_V_W), spec(GLA_K_W), sspec],
        out_specs=[spec(GLA_V_W), sspec],
        out_shape=[jax.ShapeDtypeStruct((B, S, GLA_V_W), F32),
                   jax.ShapeDtypeStruct((B, GLA_HEADS, GLA_DK, GLA_DV), F32)],
        scratch_shapes=[pltpu.VMEM((bt, GLA_V_W, GLA_K_W), F32)],
        compiler_params=_params(2),
        name="gla_scan",
    )(qg, kg, vg, la, s0)


DECODE_SUBSTEPS = 8
MLP_CHUNKS = 4


def _finish_stages(x_ref, of_ref, og_ref, g1_ref, gn_ref, g2_ref, g3_ref, g4_ref, wog_ref, wgf_ref, wgg_ref, wfo_ref,
                   wgo_ref, wo_ref, wu_ref, wd_ref, y_ref):
    x = x_ref[...]
    h = _rms(x, g1_ref[...]).astype(BF16)
    gate = _dot(h, wog_ref[...])
    gf = _dot(h, wgf_ref[...])
    yield
    gg = _dot(h, wgg_ref[...])
    fo = _dot(of_ref[...], wfo_ref[...])
    o = og_ref[...]
    ys = []
    for hh in range(GLA_HEADS):
        sl = slice(hh * GLA_DV, (hh + 1) * GLA_DV)
        gt = gate[:, sl]
        ys.append(_rms(o[:, sl], gn_ref[...]) * (gt * _sigmoid(gt)))
    y_gla = jnp.concatenate(ys, axis=1).astype(BF16)
    yield
    u = _sigmoid(gf) * fo + _sigmoid(gg) * _dot(y_gla, wgo_ref[...])
    x1 = x + _rms(_dot(u.astype(BF16), wo_ref[...]), g2_ref[...])
    h2 = _rms(x1, g3_ref[...]).astype(BF16)
    yield
    ff = wu_ref.shape[1] // MLP_CHUNKS
    acc = jnp.zeros(x.shape, F32)
    for c in range(MLP_CHUNKS):
        up = jnp.maximum(_dot(h2, wu_ref[:, c * ff:(c + 1) * ff]), 0.0)
        if c == MLP_CHUNKS - 1:
            yield
        acc = acc + _dot((up * up).astype(BF16), wd_ref[c * ff:(c + 1) * ff, :])
        if c < MLP_CHUNKS - 1:
            yield
    y_ref[...] = x1 + _rms(acc, g4_ref[...])
    yield


def _finish_body(*refs):
    for _ in _finish_stages(*refs):
        pass


def _finish_decode_body(pt_ref, *refs, G, NP, RING):
    fin, dec = refs[:16], refs[16:25]
    y_ref, o_ref = refs[25:27]
    scratch = refs[27:]
    i = pl.program_id(0)
    n_sub = pl.num_programs(0) * DECODE_SUBSTEPS
    stages = _finish_stages(*fin, y_ref)
    for r in range(DECODE_SUBSTEPS):
        next(stages)
        _decode_step(pt_ref, dec, o_ref, scratch, i * DECODE_SUBSTEPS + r, n_sub, r, DECODE_SUBSTEPS, G, NP, RING)


def _finish(x, o_fox, o_gla, W, tm):
    N, D = x.shape
    row = lambda w: pl.BlockSpec((tm, w), lambda i: (i, 0))
    full = lambda a: pl.BlockSpec(a.shape, lambda i: (0,) * a.ndim, pipeline_mode=pl.Buffered(1))
    consts = [W[n] for n in ("g1", "gn", "g2", "g3", "g4", "wog", "wgf", "wgg", "wfo", "wgo", "wo", "wu", "wd")]
    return pl.pallas_call(
        _finish_body,
        grid=(N // tm,),
        in_specs=[row(D), row(FOX_W), row(GLA_V_W)] + [full(a) for a in consts],
        out_specs=row(D),
        out_shape=jax.ShapeDtypeStruct((N, D), F32),
        compiler_params=_params(1),
        name="finish",
    )(x, o_fox, o_gla, *consts)


def _decode_step(pt_ref, dec, o_ref, scratch, step, n_steps, r, period, G, NP, RING):
    kc_hbm, vc_hbm, fc_hbm, q_ref, kn_ref, vn_ref, cn_ref, qrow_ref, w_ref = dec
    kbuf, vbuf, fbuf, sem, qs_ref, acc_ref, car_ref, m_ref, l_ref, qbd_ref = scratch
    ng = NP // G
    g = lax.rem(step, jnp.int32(ng))
    _, _, H, HD, PAGE = kbuf.shape
    reachable = lambda gval: (gval - r) % math.gcd(ng, period) == 0

    def page_copies(s, slot):
        bb, gg = (s // ng, s % ng) if isinstance(s, int) else (lax.div(s, jnp.int32(ng)), lax.rem(s, jnp.int32(ng)))
        out = []
        for j in range(G):
            page = pt_ref[bb, NP - 1 - (gg * G + j)]
            out.append(pltpu.make_async_copy(kc_hbm.at[0, page], kbuf.at[slot, j], sem.at[0, slot]))
            out.append(pltpu.make_async_copy(vc_hbm.at[0, page], vbuf.at[slot, j], sem.at[1, slot]))
            out.append(pltpu.make_async_copy(fc_hbm.at[0, page], fbuf.at[slot, j], sem.at[2, slot]))
        return out

    if r == 0:
        @pl.when(step == 0)
        def _():
            for s in range(RING - 1):
                for cp in page_copies(s, s):
                    cp.start()

    ahead = step + (RING - 1)
    for cp in page_copies(jnp.minimum(ahead, n_steps - 1), lax.rem(ahead, jnp.int32(RING))):
        cp.start()

    slot = lax.rem(step, jnp.int32(RING))
    for cp in page_copies(step, slot):
        cp.wait()
    k_refs = [kbuf.at[slot, j] for j in range(G)]
    v_refs = [vbuf.at[slot, j] for j in range(G)]
    f_refs = [fbuf.at[slot, j] for j in range(G)]

    if reachable(0):
        @pl.when(g == 0)
        def _():
            qs_ref[...] = jnp.broadcast_to(q_ref[...], (H, HD, PAGE))
            wide = jnp.concatenate([qrow_ref[...]] * H, axis=1)
            mine = (lax.broadcasted_iota(jnp.int32, wide.shape, 1) // HD
                    == lax.broadcasted_iota(jnp.int32, wide.shape, 0))
            qbd_ref[...] = jnp.concatenate([jnp.where(mine, wide, 0.0), jnp.zeros_like(wide)], axis=0).astype(BF16)
            car_ref[...] = jnp.broadcast_to(cn_ref[...], (H, PAGE))
            m_ref[...] = jnp.full(m_ref.shape, -jnp.inf, F32)
            l_ref[...] = jnp.zeros(l_ref.shape, F32)
            acc_ref[...] = jnp.zeros(acc_ref.shape, F32)

    lf = jnp.concatenate([f_refs[j][...] for j in range(G)], axis=0)
    w = w_ref[...]
    suf = sum(_dot(part, w) for part in _split3(lf))
    carry = car_ref[...]
    bias = []
    for j in range(G):
        bias.append(suf[j * H:(j + 1) * H, :PAGE] + carry)
        carry = carry + suf[j * H:(j + 1) * H, PAGE:]
    car_ref[...] = carry

    qbd = qbd_ref[...]
    logits = [_dot(qbd, k_refs[j][...].reshape(H * HD, PAGE).astype(BF16))[:H] + bias[j] for j in range(G)]

    m_old = m_ref[...]
    m_new = jnp.maximum(m_old, functools.reduce(jnp.maximum, logits))
    alpha = jnp.exp(m_old - m_new)
    probs = [jnp.exp(lg - m_new) for lg in logits]
    l_ref[...] = l_ref[...] * alpha + sum(probs)
    m_ref[...] = m_new
    for h in range(H):
        a = acc_ref[h] * alpha[h:h + 1, :]
        for j in range(G):
            a = a + v_refs[j][h] * probs[j][h:h + 1, :]
        acc_ref[h] = a

    if reachable(ng - 1):
        @pl.when(g == ng - 1)
        def _():
            for h in range(H):
                ln = jnp.sum(qs_ref[h] * kn_ref[h], axis=0, keepdims=True)
                m = m_ref[h:h + 1, :]
                mf = jnp.maximum(jnp.max(m, axis=1, keepdims=True), ln)
                wl = jnp.exp(m - mf)
                pn = jnp.exp(ln - mf)[:, 0:1]
                den = jnp.sum(l_ref[h:h + 1, :] * wl, axis=1, keepdims=True) + pn
                num = jnp.sum(acc_ref[h] * wl, axis=1, keepdims=True) + pn * vn_ref[h]
                o_ref[h] = num / den

    if r >= period - (RING - 1):
        @pl.when(step >= n_steps - (RING - 1))
        def _():
            for cp in page_copies(n_steps - 1, lax.rem(ahead, jnp.int32(RING))):
                cp.wait()


def _finish_decode(x, o_fox, o_gla, W, tm, kc, vc, fc, page_table, q, kn, vn, cn, qrow, G):
    N, D = x.shape
    DB, NP = page_table.shape
    _, _, H, HD, PAGE = kc.shape
    ng = NP // G
    n_tiles = N // tm
    assert ng % DECODE_SUBSTEPS == 0 and n_tiles * DECODE_SUBSTEPS == DB * ng
    tiles_per_seq = ng // DECODE_SUBSTEPS
    pos = lax.broadcasted_iota(jnp.int32, (PAGE, 2 * PAGE), 0)
    lane = lax.broadcasted_iota(jnp.int32, (PAGE, 2 * PAGE), 1)
    wsuf = jnp.where(lane < PAGE, pos > lane, True).astype(BF16)

    row = lambda w: pl.BlockSpec((tm, w), lambda i, pt: (i, 0))
    full = lambda a: pl.BlockSpec(a.shape, lambda i, pt: (0,) * a.ndim, pipeline_mode=pl.Buffered(1))
    consts = [W[n] for n in ("g1", "gn", "g2", "g3", "g4", "wog", "wgf", "wgg", "wfo", "wgo", "wo", "wu", "wd")]
    hbm = pl.BlockSpec(memory_space=pl.ANY)
    col = pl.BlockSpec((None, H, HD, 1), lambda i, pt: (i // tiles_per_seq, 0, 0, 0))
    specs = ([row(D), row(FOX_W), row(GLA_V_W)] + [full(a) for a in consts]
             + [hbm, hbm, hbm, col, col, col, pl.BlockSpec((None, H, 1), lambda i, pt: (i // tiles_per_seq, 0, 0)),
                pl.BlockSpec((None, H, HD), lambda i, pt: (i // tiles_per_seq, 0, 0)), full(wsuf)])
    return pl.pallas_call(
        functools.partial(_finish_decode_body, G=G, NP=NP, RING=DECODE_RING),
        grid_spec=pltpu.PrefetchScalarGridSpec(
            num_scalar_prefetch=1,
            grid=(n_tiles,),
            in_specs=specs,
            out_specs=[row(D), col],
            scratch_shapes=[pltpu.VMEM((DECODE_RING, G, H, HD, PAGE), F32),
                            pltpu.VMEM((DECODE_RING, G, H, HD, PAGE), F32),
                            pltpu.VMEM((DECODE_RING, G, H, PAGE), F32),
                            pltpu.SemaphoreType.DMA((3, DECODE_RING)),
                            pltpu.VMEM((H, HD, PAGE), F32), pltpu.VMEM((H, HD, PAGE), F32),
                            pltpu.VMEM((H, PAGE), F32), pltpu.VMEM((H, PAGE), F32),
                            pltpu.VMEM((H, PAGE), F32), pltpu.VMEM((2 * H, H * HD), BF16)]),
        out_shape=[jax.ShapeDtypeStruct((N, D), F32), jax.ShapeDtypeStruct((DB, H, HD, 1), F32)],
        compiler_params=_params(1),
        name="finish_decode",
    )(page_table, x, o_fox, o_gla, *consts, kc, vc, fc, q, kn, vn, cn, qrow, wsuf)


def _layer_weights(l, g_pre_mix, w_in, b_f, w_alpha_up, b_alpha, g_gla_norm, w_fox_out, w_gla_out, w_o,
                   g_post_mix, g_pre_mlp, w_up, w_down, g_post_mlp):
    sizes = (FOX_W, FOX_W, FOX_W, FOX_HEADS, GLA_K_W, GLA_K_W, GLA_V_W, GLA_V_W, GLA_RANK)
    off = [0]
    for s in sizes:
        off.append(off[-1] + s)
    w = w_in[l]
    D = w.shape[0]
    wb = w.astype(BF16)
    wqkv = wb[:, :off[3]]
    wgla = wb[:, off[4]:off[7]]
    wog = wb[:, off[7]:off[8]]
    wgf = wb[:, off[9]:off[9] + D]
    wgg = wb[:, off[9] + D:off[9] + 2 * D]
    nf = F_PARTS * FOX_HEADS
    pad = LANES - nf - GLA_RANK
    wsm = jnp.concatenate([wb[:, off[3]:off[4]]] * F_PARTS + [wb[:, off[8]:off[9]], jnp.zeros((D, pad), BF16)], axis=1)
    wau = jnp.concatenate([jnp.zeros((nf, GLA_K_W), BF16), w_alpha_up[l].astype(BF16),
                           jnp.zeros((pad, GLA_K_W), BF16)], axis=0)
    bfr = jnp.concatenate([b_f[l]] * F_PARTS + [jnp.zeros((LANES - nf,), F32)])[None, :]
    r = lambda a: a[l][None, :]
    return dict(g1=r(g_pre_mix), wqkv=wqkv, wgla=wgla, wsm=wsm, wau=wau, bal=r(b_alpha), bfr=bfr,
                gn=r(g_gla_norm), wog=wog, wgf=wgf, wgg=wgg, wfo=w_fox_out[l].astype(BF16),
                wgo=w_gla_out[l].astype(BF16), wo=w_o[l].astype(BF16), g2=r(g_post_mix), g3=r(g_pre_mlp),
                wu=w_up[l].astype(BF16), wd=w_down[l].astype(BF16), g4=r(g_post_mlp))


def kernel(x_prompt, x_sample, cache_k, cache_v, cache_logf, state_gla, page_table, g_pre_mix, w_in, b_f, w_alpha_up, b_alpha, g_gla_norm, w_fox_out, w_gla_out, w_o, g_post_mix, g_pre_mlp, w_up, w_down, g_post_mlp):
    B, S, D = x_prompt.shape
    DB = x_sample.shape[0]
    depth = w_in.shape[0]
    assert depth == 1 and x_sample.shape[1] == 1
    W = _layer_weights(0, g_pre_mix, w_in, b_f, w_alpha_up, b_alpha, g_gla_norm, w_fox_out, w_gla_out, w_o,
                       g_post_mix, g_pre_mlp, w_up, w_down, g_post_mlp)
    proj = lambda x, tm, sc: _in_proj(x, W["g1"], W["wqkv"], W["wgla"], W["wsm"], W["wau"], W["bal"], W["bfr"], tm, sc)

    PADT = LANES
    xs = jnp.zeros((1, PADT, D), F32).at[0, :DB].set(x_sample[:, 0])
    kT, vT, _, qTb, _, _, lfT, qg_s, kg_s, vg_s, la_s = proj(xs, PADT, 1.0)
    k_s = kT[0].T[:DB].reshape(DB, FOX_HEADS, FOX_HD)
    v_s = vT[0].T[:DB].reshape(DB, FOX_HEADS, FOX_HD)
    lf_s = lfT[0].T[:DB]
    q_s = qTb[0].T[:DB].astype(F32).reshape(DB, FOX_HEADS, FOX_HD)
    kc = jnp.transpose(cache_k, (0, 1, 3, 4, 2))
    vc = jnp.transpose(cache_v, (0, 1, 3, 4, 2))
    fc = jnp.transpose(cache_logf, (0, 1, 3, 2))

    kT, vT, vTb, qTb, kb, fa, lfT, qg, kg, vg, la = proj(x_prompt, ROW_TILE, LOG2E)
    o_fox = _fox_prompt(kb, fa, qTb, vTb, ATT_TILE)
    o_gla, s_p = _gla(qg, kg, vg, la, jnp.zeros((B, GLA_HEADS, GLA_DK, GLA_DV), F32), B)
    y_p, o_fs = _finish_decode(x_prompt.reshape(B * S, D), o_fox.reshape(B * S, FOX_W), o_gla.reshape(B * S, GLA_V_W),
                               W, ROW_TILE, kc, vc, fc, page_table, q_s[..., None], k_s[..., None], v_s[..., None],
                               lf_s[..., None], q_s, PAGES_PER_STEP)
    y_p = y_p.reshape(B, S, D)
    to_tok = lambda a: jnp.transpose(a.reshape(1, B, FOX_HEADS, FOX_HD, S), (0, 1, 4, 2, 3))
    k_p, v_p = to_tok(kT), to_tok(vT)
    lf_p = jnp.transpose(lfT, (0, 2, 1))[None]

    o_fox_s = jnp.zeros((PADT, FOX_W), BF16).at[:DB].set(o_fs.reshape(DB, FOX_W).astype(BF16))
    C = GLA_CHUNK
    pad_c = lambda a: jnp.zeros((DB, C, a.shape[-1]), a.dtype).at[:, 0].set(a[0, :DB])
    o_gs, s_s = _gla(pad_c(qg_s), pad_c(kg_s), pad_c(vg_s), pad_c(la_s), state_gla[0], DB // 2)
    o_gla_s = jnp.zeros((PADT, GLA_V_W), F32).at[:DB].set(o_gs[:, 0])
    y_s = _finish(xs[0], o_fox_s, o_gla_s, W, PADT)[:DB].reshape(DB, 1, D)

    return (y_p, y_s, k_p, v_p, lf_p, s_p[None],
            k_s[None, :, None], v_s[None, :, None], lf_s[None, :, None], s_s[None])
```
